```python
import jax, jax.numpy as jnp
from jax import lax
import numpy as np

D_MODEL = 1024
BATCH = 32
SEQ = 2048
DEPTH = 1

D_MIX = D_MODEL
MLA_HEADS = 4
MLA_NOPE = 128
MLA_ROPE = 64
MLA_V = 128
MLA_WIDTH = MLA_HEADS * MLA_V
Q_LORA = 256
KV_LORA = 128
ROPE_THETA = 10000.0
Q_BLOCK = 128
RW_HEAD = 64
RW_WIDTH = D_MIX - MLA_WIDTH
RW_HEADS = RW_WIDTH // RW_HEAD
W_LORA = 64
A_LORA = 64
RW_GN_EPS = 64e-5
NORM_EPS = 1e-6
MLA_COLS = Q_LORA + KV_LORA + MLA_ROPE
RW_SHIFT_COLS = 3 * RW_WIDTH + W_LORA + A_LORA
GATE_COLS = D_MIX
D_IN = MLA_COLS + RW_SHIFT_COLS + GATE_COLS

kernel_name = 'hymba_mla_rwkv7_sandwich'


def rmsnorm(x, g):
    xf = x.astype(jnp.float32)
    y = xf * lax.rsqrt(jnp.mean(xf * xf, axis=-1, keepdims=True) + NORM_EPS)
    return (y * g.astype(jnp.float32)).astype(x.dtype)


def rope_tables(positions):
    inv_freq = ROPE_THETA ** (-jnp.arange(0, MLA_ROPE, 2, dtype=jnp.float32) / MLA_ROPE)
    ang = positions.astype(jnp.float32)[..., None] * inv_freq
    ang = jnp.concatenate([ang, ang], axis=-1)
    return jnp.cos(ang), jnp.sin(ang)


def apply_rope(x, cos, sin):
    x1, x2 = jnp.split(x, 2, axis=-1)
    rot = jnp.concatenate([-x2, x1], axis=-1)
    return (x.astype(jnp.float32) * cos + rot.astype(jnp.float32) * sin).astype(x.dtype)


def token_shift(p):
    return jnp.pad(p[:, :-1], ((0, 0), (1, 0), (0, 0)))


def mla_attention(q_nope, q_rope, k_nope, k_rope, v):
    T = q_nope.shape[1]
    scale = (MLA_NOPE + MLA_ROPE) ** -0.5
    outs = []
    for i in range(T // Q_BLOCK):
        s, e = i * Q_BLOCK, (i + 1) * Q_BLOCK
        scores = (jnp.einsum('bqhd,bkhd->bhqk', q_nope[:, s:e], k_nope[:, :e])
                  + jnp.einsum('bqhr,bkr->bhqk', q_rope[:, s:e], k_rope[:, :e])).astype(jnp.float32) * scale
        mask = (s + jnp.arange(Q_BLOCK))[:, None] >= jnp.arange(e)[None, :]
        scores = jnp.where(mask, scores, -jnp.inf)
        probs = jax.nn.softmax(scores, axis=-1).astype(v.dtype)
        outs.append(jnp.einsum('bhqk,bkhd->bqhd', probs, v[:, :e]))
    return jnp.concatenate(outs, axis=1)


def wkv7_scan(r, w, k, v, kk, a):
    B, T, H, N = r.shape

    def step(S, inp):
        r_t, w_t, k_t, v_t, kk_t, a_t = inp
        sa = jnp.einsum('bhvk,bhk->bhv', S, -kk_t)
        S = (S * w_t[:, :, None, :] + sa[..., None] * (kk_t * a_t)[:, :, None, :]
             + v_t[..., None] * k_t[:, :, None, :])
        return S, jnp.einsum('bhvk,bhk->bhv', S, r_t)

    xs = tuple(jnp.moveaxis(t.astype(jnp.float32), 1, 0) for t in (r, w, k, v, kk, a))
    S0 = jnp.zeros((B, H, N, N), jnp.float32)
    _, ys = lax.scan(step, S0, xs)
    return jnp.moveaxis(ys, 0, 1)


def setup_inputs(seed: int = 0) -> dict:
    key = jax.random.key(seed)
    ks = jax.random.split(key, 24)
    L = DEPTH
    f32 = jnp.float32

    def nrm(k, shape, scale):
        return jax.random.normal(k, shape, f32) * scale

    x = nrm(ks[0], (BATCH, SEQ, D_MODEL), 1.0)
    offset = jax.random.randint(ks[1], (BATCH, 1), 0, 4096, dtype=jnp.int32)
    positions = offset + jnp.arange(SEQ, dtype=jnp.int32)[None, :]
    return {
        'x': x,
        'positions': positions,
        'norm_pre_g': 1.0 + nrm(ks[2], (L, D_MODEL), 0.02),
        'w_in': nrm(ks[3], (L, D_MODEL, D_IN), D_MODEL ** -0.5),
        'mla_q_norm_g': 1.0 + nrm(ks[4], (L, Q_LORA), 0.02),
        'mla_w_uq': nrm(ks[5], (L, Q_LORA, MLA_HEADS * (MLA_NOPE + MLA_ROPE)), Q_LORA ** -0.5),
        'mla_kv_norm_g': 1.0 + nrm(ks[6], (L, KV_LORA), 0.02),
        'mla_w_ukv': nrm(ks[7], (L, KV_LORA, MLA_HEADS * (MLA_NOPE + MLA_V)), KV_LORA ** -0.5),
        'rw_mu': jax.random.uniform(ks[8], (L, RW_SHIFT_COLS), f32),
        'rw_w0': -2.5 + nrm(ks[9], (L, RW_WIDTH), 0.5),
        'rw_w2': nrm(ks[10], (L, W_LORA, RW_WIDTH), 0.5 * W_LORA ** -0.5),
        'rw_a0': nrm(ks[11], (L, RW_WIDTH), 0.1),
        'rw_a2': nrm(ks[12], (L, A_LORA, RW_WIDTH), 0.5 * A_LORA ** -0.5),
        'rw_k_k': 0.85 + nrm(ks[13], (L, RW_WIDTH), 0.05),
        'rw_k_a': 1.0 + nrm(ks[14], (L, RW_WIDTH), 0.05),
        'rw_r_k': nrm(ks[15], (L, RW_HEADS, RW_HEAD), 0.1),
        'rw_ln_g': 1.0 + nrm(ks[16], (L, RW_WIDTH), 0.02),
        'rw_ln_b': nrm(ks[17], (L, RW_WIDTH), 0.02),
        'w_out': nrm(ks[18], (L, D_MIX, D_MODEL), D_MIX ** -0.5),
        'norm_post_g': 1.0 + nrm(ks[19], (L, D_MODEL), 0.02),
    }


def reference(x, positions, norm_pre_g, w_in, mla_q_norm_g, mla_w_uq, mla_kv_norm_g, mla_w_ukv,
              rw_mu, rw_w0, rw_w2, rw_a0, rw_a2, rw_k_k, rw_k_a, rw_r_k, rw_ln_g, rw_ln_b,
              w_out, norm_post_g):
    B, T, _ = x.shape
    f32 = jnp.float32
    cos, sin = rope_tables(positions)
    h = x
    for l in range(DEPTH):
        u = rmsnorm(h, norm_pre_g[l])
        p = u @ w_in[l]
        p_mla, p_rw, z = jnp.split(p, [MLA_COLS, MLA_COLS + RW_SHIFT_COLS], axis=-1)

        c_q, c_kv, k_r = jnp.split(p_mla, [Q_LORA, Q_LORA + KV_LORA], axis=-1)
        q = (rmsnorm(c_q, mla_q_norm_g[l]) @ mla_w_uq[l]).reshape(B, T, MLA_HEADS, MLA_NOPE + MLA_ROPE)
        q_nope, q_rope = jnp.split(q, [MLA_NOPE], axis=-1)
        kv = (rmsnorm(c_kv, mla_kv_norm_g[l]) @ mla_w_ukv[l]).reshape(B, T, MLA_HEADS, MLA_NOPE + MLA_V)
        k_nope, v_mla = jnp.split(kv, [MLA_NOPE], axis=-1)
        q_rope = apply_rope(q_rope, cos[:, :, None, :], sin[:, :, None, :])
        k_r = apply_rope(k_r, cos, sin)
        y_mla = mla_attention(q_nope, q_rope, k_nope, k_r, v_mla).reshape(B, T, MLA_WIDTH)

        ps = p_rw + (token_shift(p_rw) - p_rw) * rw_mu[l]
        r, k, v, xw, xa = jnp.split(
            ps, [RW_WIDTH, 2 * RW_WIDTH, 3 * RW_WIDTH, 3 * RW_WIDTH + W_LORA], axis=-1)
        w_log = -jax.nn.softplus(-(rw_w0[l] + jnp.tanh(xw) @ rw_w2[l]).astype(f32)) - 0.5
        decay = jnp.exp(-jnp.exp(w_log))
        a = jax.nn.sigmoid((rw_a0[l] + xa @ rw_a2[l]).astype(f32))
        kk = (k * rw_k_k[l]).astype(f32).reshape(B, T, RW_HEADS, RW_HEAD)
        kk = kk / jnp.maximum(jnp.linalg.norm(kk, axis=-1, keepdims=True), 1e-12)
        k = k.astype(f32) * (1.0 + (a - 1.0) * rw_k_a[l].astype(f32))
        heads = lambda t: t.reshape(B, T, RW_HEADS, RW_HEAD)
        r_h, k_h, v_h = heads(r.astype(f32)), heads(k), heads(v.astype(f32))
        y = wkv7_scan(r_h, heads(decay), k_h, v_h, kk, heads(a))
        mean = jnp.mean(y, axis=-1, keepdims=True)
        var = jnp.mean(jnp.square(y - mean), axis=-1, keepdims=True)
        y = ((y - mean) * lax.rsqrt(var + RW_GN_EPS)).reshape(B, T, RW_WIDTH)
        y = y * rw_ln_g[l].astype(f32) + rw_ln_b[l].astype(f32)
        bonus = jnp.sum(r_h * k_h * rw_r_k[l].astype(f32), axis=-1, keepdims=True) * v_h
        y_rw = (y + bonus.reshape(B, T, RW_WIDTH)).astype(x.dtype)

        y_cat = jnp.concatenate([y_mla, y_rw], axis=-1) * jax.nn.silu(z)
        out = y_cat @ w_out[l]
        h = h + rmsnorm(out, norm_post_g[l])
    return h
```

```python
import functools

import jax
import jax.numpy as jnp
import numpy as np
from jax import lax
from jax.experimental import pallas as pl
from jax.experimental.pallas import tpu as pltpu

D_MODEL = 1024
MLA_HEADS = 4
MLA_NOPE = 128
MLA_ROPE = 64
MLA_V = 128
MLA_WIDTH = MLA_HEADS * MLA_V
Q_LORA = 256
KV_LORA = 128
ROPE_THETA = 10000.0
RW_HEAD = 64
RW_WIDTH = 512
RW_HEADS = 8
W_LORA = 64
A_LORA = 64
RW_GN_EPS = 64e-5
NORM_EPS = 1e-6
MLA_COLS = Q_LORA + KV_LORA + MLA_ROPE
RW_SHIFT_COLS = 3 * RW_WIDTH + W_LORA + A_LORA
QK_PAD = 256
WA_COLS = Q_LORA + KV_LORA + 128 + 128

CHUNK = 64
TM_PROJ = 256
TQ = 256
TM_OUT = 512
VMEM_LIMIT = 48 * 1024 * 1024

F32 = jnp.float32
BF16 = jnp.bfloat16


def _dot(a, b):
    return jnp.dot(a.astype(BF16), b.astype(BF16), preferred_element_type=F32)


def _dot_nt(a, b):
    return lax.dot_general(a.astype(BF16), b.astype(BF16), (((1,), (1,)), ((), ())),
                           preferred_element_type=F32)


def _rms(x, g):
    return x * lax.rsqrt(jnp.mean(x * x, axis=-1, keepdims=True) + NORM_EPS) * g


def _proj_kernel(tiles_per_seq,
                 x_ref, pos_ref, gpre_ref, wa_ref, wrw_ref, wz_ref,
                 qg_ref, wqn_ref, wqr_ref, wqt_ref, kvg_ref, wkn_ref, wkv_ref, invf_ref,
                 mu_ref, w0_ref, w2_ref, a0_ref, a2_ref, kkw_ref, ka_ref, ones_ref,
                 q_ref, k_ref, v_ref, r_ref, lw_ref, kp_ref, vr_ref, kk_ref, b_ref, z_ref,
                 shift_ref):
    tm = x_ref.shape[0]
    i = pl.program_id(0)
    scale = (MLA_NOPE + MLA_ROPE) ** -0.5

    u = _rms(x_ref[...], gpre_ref[...]).astype(BF16)

    pa = jnp.dot(u, wa_ref[...], preferred_element_type=F32)
    c_q = pa[:, :Q_LORA]
    c_kv = pa[:, Q_LORA:Q_LORA + KV_LORA]
    kr = pa[:, Q_LORA + KV_LORA:Q_LORA + KV_LORA + 128]
    kr_rot = pa[:, Q_LORA + KV_LORA + 128:]

    ang = pos_ref[...] * invf_ref[...]
    cos = jnp.cos(ang)
    sin = jnp.sin(ang)

    cqn = _rms(c_q, qg_ref[...]).astype(BF16)
    qn = jnp.dot(cqn, wqn_ref[...], preferred_element_type=F32)
    qr = jnp.dot(cqn, wqr_ref[...], preferred_element_type=F32)
    qt = jnp.dot(cqn, wqt_ref[...], preferred_element_type=F32)
    ckn = _rms(c_kv, kvg_ref[...]).astype(BF16)
    kn = jnp.dot(ckn, wkn_ref[...], preferred_element_type=F32)
    vv = jnp.dot(ckn, wkv_ref[...], preferred_element_type=F32)
    k_rope = (kr * cos + kr_rot * sin).astype(BF16)
    for h in range(MLA_HEADS):
        sl = slice(128 * h, 128 * (h + 1))
        q_ref[h, :, :128] = (qn[:, sl] * scale).astype(BF16)
        q_ref[h, :, 128:] = ((qr[:, sl] * cos + qt[:, sl] * sin) * scale).astype(BF16)
        k_ref[h, :, :128] = kn[:, sl].astype(BF16)
        k_ref[h, :, 128:] = k_rope
        v_ref[h] = vv[:, sl].astype(BF16)

    z_ref[...] = jnp.dot(u, wz_ref[...], preferred_element_type=F32)

    prw = jnp.dot(u, wrw_ref[...], preferred_element_type=F32)

    @pl.when(i % tiles_per_seq == 0)
    def _():
        shift_ref[0:8, :] = jnp.zeros((8, RW_SHIFT_COLS), F32)

    @pl.when(i % tiles_per_seq != 0)
    def _():
        shift_ref[7:8, :] = shift_ref[tm + 7:tm + 8, :]

    shift_ref[8:tm + 8, :] = prw
    prev = shift_ref[7:tm + 7, :]
    ps = prw + (prev - prw) * mu_ref[...]
    r = ps[:, :RW_WIDTH]
    k = ps[:, RW_WIDTH:2 * RW_WIDTH]
    v = ps[:, 2 * RW_WIDTH:3 * RW_WIDTH]
    xw = ps[:, 3 * RW_WIDTH:3 * RW_WIDTH + W_LORA]
    xa = ps[:, 3 * RW_WIDTH + W_LORA:]

    t = -(w0_ref[...] + jnp.dot(jnp.tanh(xw).astype(BF16), w2_ref[...], preferred_element_type=F32))
    softplus = jnp.maximum(t, 0.0) + jnp.log1p(jnp.exp(-jnp.abs(t)))
    w_log = -softplus - 0.5
    lw_ref[...] = -jnp.exp(w_log)
    a_pre = a0_ref[...] + jnp.dot(xa.astype(BF16), a2_ref[...], preferred_element_type=F32)
    a = 1.0 / (1.0 + jnp.exp(-a_pre))
    kk = k * kkw_ref[...]
    sq = kk * kk
    sq_hi = sq.astype(BF16)
    sq_lo = (sq - sq_hi.astype(F32)).astype(BF16)
    ss = (jnp.dot(sq_hi, ones_ref[...], preferred_element_type=F32)
          + jnp.dot(sq_lo, ones_ref[...], preferred_element_type=F32))
    kk = kk / jnp.maximum(jnp.sqrt(ss), 1e-12)
    r_ref[...] = r
    kp_ref[...] = k * (1.0 + (a - 1.0) * ka_ref[...])
    vr_ref[...] = v
    kk_ref[...] = kk
    b_ref[...] = kk * a


def _attn_kernel(q_ref, k_ref, v_ref, o_ref):
    tq = q_ref.shape[1]
    i = pl.program_id(2)
    q = q_ref[0]

    def step(j, carry, masked):
        m, l, acc = carry
        kj = k_ref[0, pl.ds(j * tq, tq), :]
        vj = v_ref[0, pl.ds(j * tq, tq), :]
        s = lax.dot_general(q, kj, (((1,), (1,)), ((), ())), preferred_element_type=F32)
        if masked:
            row = lax.broadcasted_iota(jnp.int32, (tq, tq), 0)
            col = lax.broadcasted_iota(jnp.int32, (tq, tq), 1)
            s = jnp.where(row >= col, s, -jnp.inf)
        m_new = jnp.maximum(m, jnp.max(s, axis=-1, keepdims=True))
        alpha = jnp.exp(m - m_new)
        p = jnp.exp(s - m_new)
        l = alpha * l + jnp.sum(p, axis=-1, keepdims=True)
        acc = alpha * acc + jnp.dot(p.astype(BF16), vj, preferred_element_type=F32)
        return m_new, l, acc

    init = (jnp.full((tq, 1), -jnp.inf, F32), jnp.zeros((tq, 1), F32), jnp.zeros((tq, MLA_V), F32))
    carry = lax.fori_loop(0, i, lambda j, c: step(j, c, False), init)
    m, l, acc = step(i, carry, True)
    o_ref[...] = acc / l


def _tri_inv(a_strict, masks):
    eye, d8, offs = masks
    d = jnp.where(d8, a_strict, 0.0)
    d2 = _dot(d, d)
    d4 = _dot(d2, d2)
    t = _dot(_dot(eye - d, eye + d2), eye + d4)
    for off in offs:
        a_off = jnp.where(off, a_strict, 0.0)
        t = t - _dot(_dot(t, a_off), t)
    return t


def _wkv_kernel(r_ref, lw_ref, k_ref, v_ref, kk_ref, b_ref, tri_ref, rk_ref, g_ref, bb_ref,
                y_ref, h_ref):
    c = CHUNK
    n = RW_HEAD

    @pl.when(pl.program_id(1) == 0)
    def _():
        h_ref[...] = jnp.zeros(h_ref.shape, F32)

    lw = lw_ref[...]
    lw1 = lw.astype(BF16)
    rem = lw - lw1.astype(F32)
    lw2 = rem.astype(BF16)
    lw3 = (rem - lw2.astype(F32)).astype(BF16)
    tri = tri_ref[...]
    cum = (jnp.dot(tri, lw1, preferred_element_type=F32) + jnp.dot(tri, lw2, preferred_element_type=F32)
           + jnp.dot(tri, lw3, preferred_element_type=F32))
    cum_end = cum[c - 1:c, :]
    e_cum = jnp.exp(cum)
    e_neg = jnp.exp(-cum)
    e_prev = jnp.exp(cum - lw)
    e_bar = jnp.exp(cum_end - cum)
    p_end = jnp.exp(cum_end)

    r = r_ref[...]
    k = k_ref[...]
    v = v_ref[...]
    kk = kk_ref[...]
    b = b_ref[...]
    r_t = r * e_cum
    kap_t = kk * e_prev
    k_h = k * e_neg
    b_h = b * e_neg
    k_bar = k * e_bar
    b_bar = b * e_bar
    bonus_w = r * k * rk_ref[...]

    ii = lax.broadcasted_iota(jnp.int32, (c, c), 0)
    jj = lax.broadcasted_iota(jnp.int32, (c, c), 1)
    strict = ii > jj
    incl = ii >= jj
    eye = (ii == jj).astype(F32)
    d8 = (ii // 8) == (jj // 8)
    offs = []
    blk = 8
    while blk < c:
        offs.append(((ii // (2 * blk)) == (jj // (2 * blk))) & ((ii // blk) != (jj // blk)))
        blk *= 2
    masks = (eye, d8, offs)
    ik = lax.broadcasted_iota(jnp.int32, (n, n), 0)
    jk = lax.broadcasted_iota(jnp.int32, (n, n), 1)
    eye_n = ik == jk

    outs = []
    for h in range(RW_HEADS):
        sl = slice(n * h, n * (h + 1))
        kap_h, r_h, kh_h, bh_h = kap_t[:, sl], r_t[:, sl], k_h[:, sl], b_h[:, sl]
        v_h = v[:, sl]
        a_kb = jnp.where(strict, _dot_nt(kap_h, bh_h), 0.0)
        a_kk = jnp.where(strict, _dot_nt(kap_h, kh_h), 0.0)
        a_rk = jnp.where(incl, _dot_nt(r_h, kh_h), 0.0)
        a_rb = jnp.where(incl, _dot_nt(r_h, bh_h), 0.0)
        tinv = _tri_inv(a_kb, masks)
        w1 = _dot(tinv, kap_h)
        w2 = _dot(tinv, _dot(a_kk, v_h))
        r2 = r_h - _dot(a_rb, w1)
        y2 = _dot(a_rk, v_h) - _dot(a_rb, w2)
        bbar_t = b_bar[:, sl].T
        kbar_t = k_bar[:, sl].T
        m_mat = jnp.where(eye_n, p_end[:, sl], 0.0) - _dot(bbar_t, w1)
        g_mat = _dot(kbar_t, v_h) - _dot(bbar_t, w2)
        hs = h_ref[h]
        y = _dot(r2, hs) + y2
        h_ref[h] = _dot(m_mat, hs) + g_mat
        mean = jnp.mean(y, axis=-1, keepdims=True)
        yc = y - mean
        var = jnp.mean(yc * yc, axis=-1, keepdims=True)
        yn = yc * lax.rsqrt(var + RW_GN_EPS) * g_ref[:, sl] + bb_ref[:, sl]
        bonus = jnp.sum(bonus_w[:, sl], axis=-1, keepdims=True) * v_h
        outs.append(yn + bonus)
    y_ref[...] = jnp.concatenate(outs, axis=-1)


def _out_kernel(ym_ref, yr_ref, z_ref, x_ref, wo_ref, g_ref, o_ref):
    z = z_ref[...]
    gate = z / (1.0 + jnp.exp(-z))
    y = jnp.concatenate([ym_ref[...], yr_ref[...]], axis=-1) * gate
    out = jnp.dot(y.astype(BF16), wo_ref[...], preferred_element_type=F32)
    o_ref[...] = x_ref[...] + _rms(out, g_ref[...])


def _full(shape):
    nd = len(shape)
    return pl.BlockSpec(shape, lambda *_: (0,) * nd)


def _rot_cols(w):
    half = w.shape[-1] // 2
    return jnp.concatenate([-w[..., half:], w[..., :half]], axis=-1)


def kernel(x, positions, norm_pre_g, w_in, mla_q_norm_g, mla_w_uq, mla_kv_norm_g, mla_w_ukv,
           rw_mu, rw_w0, rw_w2, rw_a0, rw_a2, rw_k_k, rw_k_a, rw_r_k, rw_ln_g, rw_ln_b,
           w_out, norm_post_g):
    bsz, seq, _ = x.shape
    n_tok = bsz * seq
    assert norm_pre_g.shape[0] == 1
    assert seq % TM_PROJ == 0 and seq % TQ == 0 and seq % CHUNK == 0 and n_tok % TM_OUT == 0
    row = lambda p: p.reshape(1, -1).astype(F32)

    x2 = x.reshape(n_tok, D_MODEL)
    pos = positions.reshape(n_tok, 1).astype(F32)
    inv_freq = ROPE_THETA ** (-jnp.arange(0, MLA_ROPE, 2, dtype=F32) / MLA_ROPE)
    invf = jnp.tile(inv_freq, 4).reshape(1, 128)

    w = w_in[0]
    w_kr = w[:, Q_LORA + KV_LORA:MLA_COLS]
    zeros64 = jnp.zeros((D_MODEL, 64), F32)
    wa = jnp.concatenate([w[:, :Q_LORA + KV_LORA], w_kr, zeros64, _rot_cols(w_kr), zeros64], axis=1).astype(BF16)
    wrw = w[:, MLA_COLS:MLA_COLS + RW_SHIFT_COLS].astype(BF16)
    wz = w[:, MLA_COLS + RW_SHIFT_COLS:].astype(BF16)
    wq = mla_w_uq[0].reshape(Q_LORA, MLA_HEADS, MLA_NOPE + MLA_ROPE)
    wqn = wq[:, :, :MLA_NOPE].reshape(Q_LORA, MLA_HEADS * 128).astype(BF16)
    wq_rope = wq[:, :, MLA_NOPE:]
    pad64 = jnp.zeros((Q_LORA, MLA_HEADS, 64), F32)
    wqr = jnp.concatenate([wq_rope, pad64], axis=-1).reshape(Q_LORA, MLA_HEADS * 128).astype(BF16)
    wqt = jnp.concatenate([_rot_cols(wq_rope), pad64], axis=-1).reshape(Q_LORA, MLA_HEADS * 128).astype(BF16)
    wkv = mla_w_ukv[0].reshape(KV_LORA, MLA_HEADS, MLA_NOPE + MLA_V)
    wkn = wkv[:, :, :MLA_NOPE].reshape(KV_LORA, MLA_HEADS * 128).astype(BF16)
    wkvv = wkv[:, :, MLA_NOPE:].reshape(KV_LORA, MLA_HEADS * 128).astype(BF16)
    head_id = np.arange(RW_WIDTH) // RW_HEAD
    ones_bd = jnp.asarray(head_id[:, None] == head_id[None, :], dtype=BF16)

    tm = TM_PROJ
    tok = lambda cols: pl.BlockSpec((tm, cols), lambda i: (i, 0))
    head_major = lambda cols: pl.BlockSpec((MLA_HEADS, tm, cols), lambda i: (0, i, 0))
    proj_in = [x2, pos, row(norm_pre_g), wa, wrw, wz,
               row(mla_q_norm_g), wqn, wqr, wqt, row(mla_kv_norm_g), wkn, wkvv, invf,
               row(rw_mu), row(rw_w0), rw_w2[0].astype(BF16), row(rw_a0), rw_a2[0].astype(BF16),
               row(rw_k_k), row(rw_k_a), ones_bd]
    proj_in_specs = [tok(D_MODEL), tok(1)] + [_full(a.shape) for a in proj_in[2:]]
    rw_shape = jax.ShapeDtypeStruct((n_tok, RW_WIDTH), F32)
    q, k, v, r, lw, kp, vr, kk, b, z = pl.pallas_call(
        functools.partial(_proj_kernel, seq // tm),
        grid=(n_tok // tm,),
        in_specs=proj_in_specs,
        out_specs=[head_major(QK_PAD), head_major(QK_PAD), head_major(MLA_V)]
                  + [tok(RW_WIDTH)] * 6 + [tok(D_MODEL)],
        out_shape=[jax.ShapeDtypeStruct((MLA_HEADS, n_tok, QK_PAD), BF16),
                   jax.ShapeDtypeStruct((MLA_HEADS, n_tok, QK_PAD), BF16),
                   jax.ShapeDtypeStruct((MLA_HEADS, n_tok, MLA_V), BF16)]
                  + [rw_shape] * 6 + [jax.ShapeDtypeStruct((n_tok, D_MODEL), F32)],
        scratch_shapes=[pltpu.VMEM((tm + 8, RW_SHIFT_COLS), F32)],
        compiler_params=pltpu.CompilerParams(dimension_semantics=("arbitrary",),
                                             vmem_limit_bytes=VMEM_LIMIT),
        name="proj",
    )(*proj_in)

    nq = seq // TQ
    y_mla = pl.pallas_call(
        _attn_kernel,
        grid=(bsz, MLA_HEADS, nq),
        in_specs=[pl.BlockSpec((1, TQ, QK_PAD), lambda bi, h, i: (h, bi * nq + i, 0)),
                  pl.BlockSpec((1, seq, QK_PAD), lambda bi, h, i: (h, bi, 0)),
                  pl.BlockSpec((1, seq, MLA_V), lambda bi, h, i: (h, bi, 0))],
        out_specs=pl.BlockSpec((TQ, MLA_V), lambda bi, h, i: (bi * nq + i, h)),
        out_shape=jax.ShapeDtypeStruct((n_tok, MLA_WIDTH), F32),
        compiler_params=pltpu.CompilerParams(dimension_semantics=("arbitrary",) * 3,
                                             vmem_limit_bytes=VMEM_LIMIT),
        name="attn",
    )(q, k, v)

    nchunk = seq // CHUNK
    tri = jnp.asarray(np.tril(np.ones((CHUNK, CHUNK), np.float32)), dtype=BF16)
    chunk_spec = pl.BlockSpec((CHUNK, RW_WIDTH), lambda bi, ci: (bi * nchunk + ci, 0))
    wkv_in = [r, lw, kp, vr, kk, b, tri, row(rw_r_k), row(rw_ln_g), row(rw_ln_b)]
    y_rw = pl.pallas_call(
        _wkv_kernel,
        grid=(bsz, nchunk),
        in_specs=[chunk_spec] * 6 + [_full(a.shape) for a in wkv_in[6:]],
        out_specs=chunk_spec,
        out_shape=rw_shape,
        scratch_shapes=[pltpu.VMEM((RW_HEADS, RW_HEAD, RW_HEAD), F32)],
        compiler_params=pltpu.CompilerParams(dimension_semantics=("arbitrary", "arbitrary"),
                                             vmem_limit_bytes=VMEM_LIMIT),
        name="wkv",
    )(*wkv_in)

    tmo = TM_OUT
    toko = lambda cols: pl.BlockSpec((tmo, cols), lambda i: (i, 0))
    out = pl.pallas_call(
        _out_kernel,
        grid=(n_tok // tmo,),
        in_specs=[toko(MLA_WIDTH), toko(RW_WIDTH), toko(D_MODEL), toko(D_MODEL),
                  _full((D_MODEL, D_MODEL)), _full((1, D_MODEL))],
        out_specs=toko(D_MODEL),
        out_shape=jax.ShapeDtypeStruct((n_tok, D_MODEL), F32),
        compiler_params=pltpu.CompilerParams(dimension_semantics=("arbitrary",),
                                             vmem_limit_bytes=VMEM_LIMIT),
        name="outproj",
    )(y_mla, y_rw, z, x2, w_out[0].astype(BF16), row(norm_post_g))
    return out.reshape(bsz, seq, D_MODEL)
```

```python
import functools

import jax
import jax.numpy as jnp
import numpy as np
from jax import lax
from jax.experimental import pallas as pl
from jax.experimental.pallas import tpu as pltpu

D_MODEL = 1024
MLA_HEADS = 4
MLA_NOPE = 128
MLA_ROPE = 64
MLA_V = 128
MLA_WIDTH = MLA_HEADS * MLA_V
Q_LORA = 256
KV_LORA = 128
ROPE_THETA = 10000.0
RW_HEAD = 64
RW_WIDTH = 512
RW_HEADS = 8
W_LORA = 64
A_LORA = 64
RW_GN_EPS = 64e-5
NORM_EPS = 1e-6
MLA_COLS = Q_LORA + KV_LORA + MLA_ROPE
RW_SHIFT_COLS = 3 * RW_WIDTH + W_LORA + A_LORA
QK_PAD = 256
WA_COLS = Q_LORA + KV_LORA + 128 + 128

CHUNK = 64
TM_WKV = 256
TM_PROJ = 256
TQ = 256
TM_OUT = 512
VMEM_LIMIT = 48 * 1024 * 1024

F32 = jnp.float32
BF16 = jnp.bfloat16


def _dot(a, b):
    return jnp.dot(a.astype(BF16), b.astype(BF16), preferred_element_type=F32)


def _dot_nt(a, b):
    return lax.dot_general(a.astype(BF16), b.astype(BF16), (((1,), (1,)), ((), ())),
                           preferred_element_type=F32)


def _rms(x, g):
    return x * lax.rsqrt(jnp.mean(x * x, axis=-1, keepdims=True) + NORM_EPS) * g


def _proj_kernel(tiles_per_seq,
                 x_ref, pos_ref, gpre_ref, wa_ref, wrw_ref, wz_ref,
                 qg_ref, wqn_ref, wqr_ref, wqt_ref, kvg_ref, wkn_ref, wkv_ref, invf_ref,
                 mu_ref, w0_ref, w2_ref, a0_ref, a2_ref, kkw_ref, ka_ref, ones_ref,
                 q_ref, k_ref, v_ref, r_ref, lw_ref, kp_ref, vr_ref, kk_ref, b_ref, z_ref,
                 shift_ref):
    tm = x_ref.shape[0]
    i = pl.program_id(0)
    scale = (MLA_NOPE + MLA_ROPE) ** -0.5

    u = _rms(x_ref[...], gpre_ref[...]).astype(BF16)

    pa = jnp.dot(u, wa_ref[...], preferred_element_type=F32)
    c_q = pa[:, :Q_LORA]
    c_kv = pa[:, Q_LORA:Q_LORA + KV_LORA]
    kr = pa[:, Q_LORA + KV_LORA:Q_LORA + KV_LORA + 128]
    kr_rot = pa[:, Q_LORA + KV_LORA + 128:]

    ang = pos_ref[...] * invf_ref[...]
    cos = jnp.cos(ang)
    sin = jnp.sin(ang)

    cqn = _rms(c_q, qg_ref[...]).astype(BF16)
    qn = jnp.dot(cqn, wqn_ref[...], preferred_element_type=F32)
    qr = jnp.dot(cqn, wqr_ref[...], preferred_element_type=F32)
    qt = jnp.dot(cqn, wqt_ref[...], preferred_element_type=F32)
    ckn = _rms(c_kv, kvg_ref[...]).astype(BF16)
    kn = jnp.dot(ckn, wkn_ref[...], preferred_element_type=F32)
    vv = jnp.dot(ckn, wkv_ref[...], preferred_element_type=F32)
    k_rope = (kr * cos + kr_rot * sin).astype(BF16)
    for h in range(MLA_HEADS):
        sl = slice(128 * h, 128 * (h + 1))
        q_ref[h, :, :128] = (qn[:, sl] * scale).astype(BF16)
        q_ref[h, :, 128:] = ((qr[:, sl] * cos + qt[:, sl] * sin) * scale).astype(BF16)
        k_ref[h, :, :128] = kn[:, sl].astype(BF16)
        k_ref[h, :, 128:] = k_rope
        v_ref[h] = vv[:, sl].astype(BF16)

    z_ref[...] = jnp.dot(u, wz_ref[...], preferred_element_type=F32)

    prw = jnp.dot(u, wrw_ref[...], preferred_element_type=F32)

    @pl.when(i % tiles_per_seq == 0)
    def _():
        shift_ref[0:8, :] = jnp.zeros((8, RW_SHIFT_COLS), F32)

    @pl.when(i % tiles_per_seq != 0)
    def _():
        shift_ref[7:8, :] = shift_ref[tm + 7:tm + 8, :]

    shift_ref[8:tm + 8, :] = prw
    prev = shift_ref[7:tm + 7, :]
    ps = prw + (prev - prw) * mu_ref[...]
    r = ps[:, :RW_WIDTH]
    k = ps[:, RW_WIDTH:2 * RW_WIDTH]
    v = ps[:, 2 * RW_WIDTH:3 * RW_WIDTH]
    xw = ps[:, 3 * RW_WIDTH:3 * RW_WIDTH + W_LORA]
    xa = ps[:, 3 * RW_WIDTH + W_LORA:]

    t = -(w0_ref[...] + jnp.dot(jnp.tanh(xw).astype(BF16), w2_ref[...], preferred_element_type=F32))
    softplus = jnp.maximum(t, 0.0) + jnp.log1p(jnp.exp(-jnp.abs(t)))
    w_log = -softplus - 0.5
    lw_ref[...] = -jnp.exp(w_log)
    a_pre = a0_ref[...] + jnp.dot(xa.astype(BF16), a2_ref[...], preferred_element_type=F32)
    a = 1.0 / (1.0 + jnp.exp(-a_pre))
    kk = k * kkw_ref[...]
    sq = kk * kk
    sq_hi = sq.astype(BF16)
    sq_lo = (sq - sq_hi.astype(F32)).astype(BF16)
    ss = (jnp.dot(sq_hi, ones_ref[...], preferred_element_type=F32)
          + jnp.dot(sq_lo, ones_ref[...], preferred_element_type=F32))
    kk = kk / jnp.maximum(jnp.sqrt(ss), 1e-12)
    r_ref[...] = r
    kp_ref[...] = k * (1.0 + (a - 1.0) * ka_ref[...])
    vr_ref[...] = v
    kk_ref[...] = kk
    b_ref[...] = kk * a


def _attn_kernel(q_ref, k_ref, v_ref, o_ref):
    tq = q_ref.shape[1]
    i = pl.program_id(2)
    q = q_ref[0]

    def step(j, carry, masked):
        m, l, acc = carry
        kj = k_ref[0, pl.ds(j * tq, tq), :]
        vj = v_ref[0, pl.ds(j * tq, tq), :]
        s = lax.dot_general(q, kj, (((1,), (1,)), ((), ())), preferred_element_type=F32)
        if masked:
            row = lax.broadcasted_iota(jnp.int32, (tq, tq), 0)
            col = lax.broadcasted_iota(jnp.int32, (tq, tq), 1)
            s = jnp.where(row >= col, s, -jnp.inf)
        m_new = jnp.maximum(m, jnp.max(s, axis=-1, keepdims=True))
        alpha = jnp.exp(m - m_new)
        p = jnp.exp(s - m_new)
        l = alpha * l + jnp.sum(p, axis=-1, keepdims=True)
        acc = alpha * acc + jnp.dot(p.astype(BF16), vj, preferred_element_type=F32)
        return m_new, l, acc

    init = (jnp.full((tq, 1), -jnp.inf, F32), jnp.zeros((tq, 1), F32), jnp.zeros((tq, MLA_V), F32))
    carry = lax.fori_loop(0, i, lambda j, c: step(j, c, False), init)
    m, l, acc = step(i, carry, True)
    o_ref[...] = acc / l


def _bdot(a, b):
    return jnp.dot(a, b, preferred_element_type=F32)


def _bdot_nt(a, b):
    return lax.dot_general(a, b, (((1,), (1,)), ((), ())), preferred_element_type=F32)


def _wkv_kernel(r_ref, lw_ref, k_ref, v_ref, kk_ref, b_ref, tri_ref, rk_ref, g_ref, bb_ref,
                y_ref, h_ref):
    c = CHUNK
    n = RW_HEAD
    tm = r_ref.shape[0]
    nc = tm // c
    probs = [(ci, h) for ci in range(nc) for h in range(RW_HEADS)]
    bf = lambda x: x.astype(BF16)

    @pl.when(pl.program_id(1) == 0)
    def _():
        h_ref[...] = jnp.zeros(h_ref.shape, F32)

    lw = lw_ref[...]
    lw1 = lw.astype(BF16)
    rem = lw - lw1.astype(F32)
    lw2 = rem.astype(BF16)
    lw3 = (rem - lw2.astype(F32)).astype(BF16)
    tri = tri_ref[...]
    cum = _bdot(tri, lw1) + _bdot(tri, lw2) + _bdot(tri, lw3)

    r = r_ref[...]
    k = k_ref[...]
    v = v_ref[...]
    kk = kk_ref[...]
    b = b_ref[...]
    e_cum = jnp.exp(cum)
    e_neg = jnp.exp(-cum)
    r_t = r * e_cum
    kap_t = kk * jnp.exp(cum - lw)
    k_h = bf(k * e_neg)
    b_h = bf(b * e_neg)
    v_b = bf(v)
    bonus_w = r * k * rk_ref[...]
    x_cat, k_bar, b_bar, p_end = [], [], [], []
    for ci in range(nc):
        rows = slice(ci * c, (ci + 1) * c)
        cum_end = cum[(ci + 1) * c - 1:(ci + 1) * c, :]
        e_bar = jnp.exp(cum_end - cum[rows])
        k_bar.append(bf(k[rows] * e_bar))
        b_bar.append(bf(b[rows] * e_bar))
        p_end.append(jnp.exp(cum_end))
        x_cat.append(bf(jnp.concatenate([kap_t[rows], r_t[rows]], axis=0)))

    ii = lax.broadcasted_iota(jnp.int32, (c, c), 0)
    jj = lax.broadcasted_iota(jnp.int32, (c, c), 1)
    i2 = lax.broadcasted_iota(jnp.int32, (2 * c, c), 0)
    j2 = lax.broadcasted_iota(jnp.int32, (2 * c, c), 1)
    low2 = ((i2 < c) & (i2 > j2)) | (i2 - c >= j2)
    eye = (ii == jj).astype(F32)
    d8 = (ii // 8) == (jj // 8)
    offs = []
    blk = 8
    while blk < c:
        offs.append(((ii // (2 * blk)) == (jj // (2 * blk))) & ((ii // blk) != (jj // blk)))
        blk *= 2
    eye_n = (lax.broadcasted_iota(jnp.int32, (n, n), 0) == lax.broadcasted_iota(jnp.int32, (n, n), 1))

    def hsl(h):
        return slice(n * h, n * (h + 1))

    def rsl(ci):
        return slice(ci * c, (ci + 1) * c)

    a_b = {p: jnp.where(low2, _bdot_nt(x_cat[p[0]][:, hsl(p[1])], b_h[rsl(p[0]), hsl(p[1])]), 0.0) for p in probs}
    a_k = {p: bf(jnp.where(low2, _bdot_nt(x_cat[p[0]][:, hsl(p[1])], k_h[rsl(p[0]), hsl(p[1])]), 0.0)) for p in probs}
    a_kb = {p: a_b[p][:c] for p in probs}
    a_rb = {p: bf(a_b[p][c:]) for p in probs}

    d = {p: jnp.where(d8, a_kb[p], 0.0) for p in probs}
    db = {p: bf(d[p]) for p in probs}
    d2 = {p: _bdot(db[p], db[p]) for p in probs}
    d2b = {p: bf(d2[p]) for p in probs}
    d4 = {p: _bdot(d2b[p], d2b[p]) for p in probs}
    t = {p: _bdot(bf(eye - d[p]), bf(eye + d2[p])) for p in probs}
    t = {p: _bdot(bf(t[p]), bf(eye + d4[p])) for p in probs}
    for off in offs:
        tb = {p: bf(t[p]) for p in probs}
        ta = {p: _bdot(tb[p], bf(jnp.where(off, a_kb[p], 0.0))) for p in probs}
        t = {p: t[p] - _bdot(bf(ta[p]), tb[p]) for p in probs}
    tb = {p: bf(t[p]) for p in probs}

    akv = {p: _bdot(a_k[p], v_b[rsl(p[0]), hsl(p[1])]) for p in probs}
    w1 = {p: bf(_bdot(tb[p], x_cat[p[0]][:c, hsl(p[1])])) for p in probs}
    w2 = {p: bf(_bdot(tb[p], bf(akv[p][:c]))) for p in probs}
    r2 = {p: bf(r_t[rsl(p[0]), hsl(p[1])] - _bdot(a_rb[p], w1[p])) for p in probs}
    y2 = {p: akv[p][c:] - _bdot(a_rb[p], w2[p]) for p in probs}
    bbar_t = {p: b_bar[p[0]][:, hsl(p[1])].T for p in probs}
    kbar_t = {p: k_bar[p[0]][:, hsl(p[1])].T for p in probs}
    m_mat = {p: bf(jnp.where(eye_n, p_end[p[0]][:, hsl(p[1])], 0.0) - _bdot(bbar_t[p], w1[p])) for p in probs}
    g_mat = {p: _bdot(kbar_t[p], v_b[rsl(p[0]), hsl(p[1])]) - _bdot(bbar_t[p], w2[p]) for p in probs}

    hs = [h_ref[h] for h in range(RW_HEADS)]
    y = {}
    for ci in range(nc):
        hb = [bf(x) for x in hs]
        for h in range(RW_HEADS):
            y[(ci, h)] = _bdot(r2[(ci, h)], hb[h]) + y2[(ci, h)]
        hs = [_bdot(m_mat[(ci, h)], hb[h]) + g_mat[(ci, h)] for h in range(RW_HEADS)]
    for h in range(RW_HEADS):
        h_ref[h] = hs[h]

    for ci in range(nc):
        outs = []
        for h in range(RW_HEADS):
            yy = y[(ci, h)]
            mean = jnp.mean(yy, axis=-1, keepdims=True)
            yc = yy - mean
            var = jnp.mean(yc * yc, axis=-1, keepdims=True)
            yn = yc * lax.rsqrt(var + RW_GN_EPS) * g_ref[:, hsl(h)] + bb_ref[:, hsl(h)]
            bonus = jnp.sum(bonus_w[rsl(ci), hsl(h)], axis=-1, keepdims=True) * v[rsl(ci), hsl(h)]
            outs.append(yn + bonus)
        y_ref[rsl(ci), :] = jnp.concatenate(outs, axis=-1)


def _out_kernel(ym_ref, yr_ref, z_ref, x_ref, wo_ref, g_ref, o_ref):
    z = z_ref[...]
    gate = z / (1.0 + jnp.exp(-z))
    y = jnp.concatenate([ym_ref[...], yr_ref[...]], axis=-1) * gate
    out = jnp.dot(y.astype(BF16), wo_ref[...], preferred_element_type=F32)
    o_ref[...] = x_ref[...] + _rms(out, g_ref[...])


def _full(shape):
    nd = len(shape)
    return pl.BlockSpec(shape, lambda *_: (0,) * nd)


def _rot_cols(w):
    half = w.shape[-1] // 2
    return jnp.concatenate([-w[..., half:], w[..., :half]], axis=-1)


def kernel(x, positions, norm_pre_g, w_in, mla_q_norm_g, mla_w_uq, mla_kv_norm_g, mla_w_ukv,
           rw_mu, rw_w0, rw_w2, rw_a0, rw_a2, rw_k_k, rw_k_a, rw_r_k, rw_ln_g, rw_ln_b,
           w_out, norm_post_g):
    bsz, seq, _ = x.shape
    n_tok = bsz * seq
    assert norm_pre_g.shape[0] == 1
    assert seq % TM_PROJ == 0 and seq % TQ == 0 and seq % TM_WKV == 0 and n_tok % TM_OUT == 0
    row = lambda p: p.reshape(1, -1).astype(F32)

    x2 = x.reshape(n_tok, D_MODEL)
    pos = positions.reshape(n_tok, 1).astype(F32)
    inv_freq = ROPE_THETA ** (-jnp.arange(0, MLA_ROPE, 2, dtype=F32) / MLA_ROPE)
    invf = jnp.tile(inv_freq, 4).reshape(1, 128)

    w = w_in[0]
    w_kr = w[:, Q_LORA + KV_LORA:MLA_COLS]
    zeros64 = jnp.zeros((D_MODEL, 64), F32)
    wa = jnp.concatenate([w[:, :Q_LORA + KV_LORA], w_kr, zeros64, _rot_cols(w_kr), zeros64], axis=1).astype(BF16)
    wrw = w[:, MLA_COLS:MLA_COLS + RW_SHIFT_COLS].astype(BF16)
    wz = w[:, MLA_COLS + RW_SHIFT_COLS:].astype(BF16)
    wq = mla_w_uq[0].reshape(Q_LORA, MLA_HEADS, MLA_NOPE + MLA_ROPE)
    wqn = wq[:, :, :MLA_NOPE].reshape(Q_LORA, MLA_HEADS * 128).astype(BF16)
    wq_rope = wq[:, :, MLA_NOPE:]
    pad64 = jnp.zeros((Q_LORA, MLA_HEADS, 64), F32)
    wqr = jnp.concatenate([wq_rope, pad64], axis=-1).reshape(Q_LORA, MLA_HEADS * 128).astype(BF16)
    wqt = jnp.concatenate([_rot_cols(wq_rope), pad64], axis=-1).reshape(Q_LORA, MLA_HEADS * 128).astype(BF16)
    wkv = mla_w_ukv[0].reshape(KV_LORA, MLA_HEADS, MLA_NOPE + MLA_V)
    wkn = wkv[:, :, :MLA_NOPE].reshape(KV_LORA, MLA_HEADS * 128).astype(BF16)
    wkvv = wkv[:, :, MLA_NOPE:].reshape(KV_LORA, MLA_HEADS * 128).astype(BF16)
    head_id = np.arange(RW_WIDTH) // RW_HEAD
    ones_bd = jnp.asarray(head_id[:, None] == head_id[None, :], dtype=BF16)

    tm = TM_PROJ
    tok = lambda cols: pl.BlockSpec((tm, cols), lambda i: (i, 0))
    head_major = lambda cols: pl.BlockSpec((MLA_HEADS, tm, cols), lambda i: (0, i, 0))
    proj_in = [x2, pos, row(norm_pre_g), wa, wrw, wz,
               row(mla_q_norm_g), wqn, wqr, wqt, row(mla_kv_norm_g), wkn, wkvv, invf,
               row(rw_mu), row(rw_w0), rw_w2[0].astype(BF16), row(rw_a0), rw_a2[0].astype(BF16),
               row(rw_k_k), row(rw_k_a), ones_bd]
    proj_in_specs = [tok(D_MODEL), tok(1)] + [_full(a.shape) for a in proj_in[2:]]
    rw_shape = jax.ShapeDtypeStruct((n_tok, RW_WIDTH), F32)
    q, k, v, r, lw, kp, vr, kk, b, z = pl.pallas_call(
        functools.partial(_proj_kernel, seq // tm),
        grid=(n_tok // tm,),
        in_specs=proj_in_specs,
        out_specs=[head_major(QK_PAD), head_major(QK_PAD), head_major(MLA_V)]
                  + [tok(RW_WIDTH)] * 6 + [tok(D_MODEL)],
        out_shape=[jax.ShapeDtypeStruct((MLA_HEADS, n_tok, QK_PAD), BF16),
                   jax.ShapeDtypeStruct((MLA_HEADS, n_tok, QK_PAD), BF16),
                   jax.ShapeDtypeStruct((MLA_HEADS, n_tok, MLA_V), BF16)]
                  + [rw_shape] * 6 + [jax.ShapeDtypeStruct((n_tok, D_MODEL), F32)],
        scratch_shapes=[pltpu.VMEM((tm + 8, RW_SHIFT_COLS), F32)],
        compiler_params=pltpu.CompilerParams(dimension_semantics=("arbitrary",),
                                             vmem_limit_bytes=VMEM_LIMIT),
        name="proj",
    )(*proj_in)

    nq = seq // TQ
    y_mla = pl.pallas_call(
        _attn_kernel,
        grid=(bsz, MLA_HEADS, nq),
        in_specs=[pl.BlockSpec((1, TQ, QK_PAD), lambda bi, h, i: (h, bi * nq + i, 0)),
                  pl.BlockSpec((1, seq, QK_PAD), lambda bi, h, i: (h, bi, 0)),
                  pl.BlockSpec((1, seq, MLA_V), lambda bi, h, i: (h, bi, 0))],
        out_specs=pl.BlockSpec((TQ, MLA_V), lambda bi, h, i: (bi * nq + i, h)),
        out_shape=jax.ShapeDtypeStruct((n_tok, MLA_WIDTH), F32),
        compiler_params=pltpu.CompilerParams(dimension_semantics=("arbitrary",) * 3,
                                             vmem_limit_bytes=VMEM_LIMIT),
        name="attn",
    )(q, k, v)

    nt = seq // TM_WKV
    tok_id = np.arange(TM_WKV)
    tri = jnp.asarray((tok_id[:, None] >= tok_id[None, :]) & (tok_id[:, None] // CHUNK == tok_id[None, :] // CHUNK),
                      dtype=BF16)
    tile_spec = pl.BlockSpec((TM_WKV, RW_WIDTH), lambda bi, ti: (bi * nt + ti, 0))
    wkv_in = [r, lw, kp, vr, kk, b, tri, row(rw_r_k), row(rw_ln_g), row(rw_ln_b)]
    y_rw = pl.pallas_call(
        _wkv_kernel,
        grid=(bsz, nt),
        in_specs=[tile_spec] * 6 + [_full(a.shape) for a in wkv_in[6:]],
        out_specs=tile_spec,
        out_shape=rw_shape,
        scratch_shapes=[pltpu.VMEM((RW_HEADS, RW_HEAD, RW_HEAD), F32)],
        compiler_params=pltpu.CompilerParams(dimension_semantics=("arbitrary", "arbitrary"),
                                             vmem_limit_bytes=VMEM_LIMIT),
        name="wkv",
    )(*wkv_in)

    tmo = TM_OUT
    toko = lambda cols: pl.BlockSpec((tmo, cols), lambda i: (i, 0))
    out = pl.pallas_call(
        _out_kernel,
        grid=(n_tok // tmo,),
        in_specs=[toko(MLA_WIDTH), toko(RW_WIDTH), toko(D_MODEL), toko(D_MODEL),
                  _full((D_MODEL, D_MODEL)), _full((1, D_MODEL))],
        out_specs=toko(D_MODEL),
        out_shape=jax.ShapeDtypeStruct((n_tok, D_MODEL), F32),
        compiler_params=pltpu.CompilerParams(dimension_semantics=("arbitrary",),
                                             vmem_limit_bytes=VMEM_LIMIT),
        name="outproj",
    )(y_mla, y_rw, z, x2, w_out[0].astype(BF16), row(norm_post_g))
    return out.reshape(bsz, seq, D_MODEL)
```

```python
import functools

import jax
import jax.numpy as jnp
import numpy as np
from jax import lax
from jax.experimental import pallas as pl
from jax.experimental.pallas import tpu as pltpu

D_MODEL = 1024
MLA_HEADS = 4
MLA_NOPE = 128
MLA_ROPE = 64
MLA_V = 128
MLA_WIDTH = MLA_HEADS * MLA_V
Q_LORA = 256
KV_LORA = 128
ROPE_THETA = 10000.0
RW_HEAD = 64
RW_WIDTH = 512
RW_HEADS = 8
W_LORA = 64
A_LORA = 64
RW_GN_EPS = 64e-5
NORM_EPS = 1e-6
MLA_COLS = Q_LORA + KV_LORA + MLA_ROPE
RW_SHIFT_COLS = 3 * RW_WIDTH + W_LORA + A_LORA
QK_PAD = 256
WA_COLS = Q_LORA + KV_LORA + 128 + 128

CHUNK = 64
TM_WKV = 256
TM_PROJ = 512
TQ = 256
TM_OUT = 512
VMEM_LIMIT = 48 * 1024 * 1024

F32 = jnp.float32
BF16 = jnp.bfloat16


def _dot(a, b):
    return jnp.dot(a.astype(BF16), b.astype(BF16), preferred_element_type=F32)


def _dot_nt(a, b):
    return lax.dot_general(a.astype(BF16), b.astype(BF16), (((1,), (1,)), ((), ())),
                           preferred_element_type=F32)


def _rms(x, g):
    return x * lax.rsqrt(jnp.mean(x * x, axis=-1, keepdims=True) + NORM_EPS) * g


def _proj_kernel(tiles_per_seq,
                 x_ref, pos_ref, gpre_ref, wa_ref, wrw_ref, wz_ref,
                 qg_ref, wqn_ref, wqr_ref, wqt_ref, kvg_ref, wkn_ref, wkv_ref, invf_ref,
                 mu_ref, w0_ref, w2_ref, a0_ref, a2_ref, kkw_ref, ka_ref, ones_ref,
                 q_ref, k_ref, v_ref, r_ref, lw_ref, kp_ref, vr_ref, kk_ref, b_ref, z_ref,
                 shift_ref):
    tm = x_ref.shape[0]
    i = pl.program_id(0)
    scale = float((MLA_NOPE + MLA_ROPE) ** -0.5 * np.log2(np.e))

    u = _rms(x_ref[...], gpre_ref[...]).astype(BF16)

    pa = jnp.dot(u, wa_ref[...], preferred_element_type=F32)
    c_q = pa[:, :Q_LORA]
    c_kv = pa[:, Q_LORA:Q_LORA + KV_LORA]
    kr = pa[:, Q_LORA + KV_LORA:Q_LORA + KV_LORA + 128]
    kr_rot = pa[:, Q_LORA + KV_LORA + 128:]

    ang = pos_ref[...] * invf_ref[...]
    cos = jnp.cos(ang)
    sin = jnp.sin(ang)

    cqn = _rms(c_q, qg_ref[...]).astype(BF16)
    qn = jnp.dot(cqn, wqn_ref[...], preferred_element_type=F32)
    qr = jnp.dot(cqn, wqr_ref[...], preferred_element_type=F32)
    qt = jnp.dot(cqn, wqt_ref[...], preferred_element_type=F32)
    ckn = _rms(c_kv, kvg_ref[...]).astype(BF16)
    kn = jnp.dot(ckn, wkn_ref[...], preferred_element_type=F32)
    vv = jnp.dot(ckn, wkv_ref[...], preferred_element_type=F32)
    k_rope = (kr * cos + kr_rot * sin).astype(BF16)
    for h in range(MLA_HEADS):
        sl = slice(128 * h, 128 * (h + 1))
        q_ref[h, :, :128] = (qn[:, sl] * scale).astype(BF16)
        q_ref[h, :, 128:] = ((qr[:, sl] * cos + qt[:, sl] * sin) * scale).astype(BF16)
        k_ref[h, :, :128] = kn[:, sl].astype(BF16)
        k_ref[h, :, 128:] = k_rope
        v_ref[h] = vv[:, sl].astype(BF16)

    z_ref[...] = jnp.dot(u, wz_ref[...], preferred_element_type=F32)

    prw = jnp.dot(u, wrw_ref[...], preferred_element_type=F32)

    @pl.when(i % tiles_per_seq == 0)
    def _():
        shift_ref[0:8, :] = jnp.zeros((8, RW_SHIFT_COLS), F32)

    @pl.when(i % tiles_per_seq != 0)
    def _():
        shift_ref[7:8, :] = shift_ref[tm + 7:tm + 8, :]

    shift_ref[8:tm + 8, :] = prw
    prev = shift_ref[7:tm + 7, :]
    ps = prw + (prev - prw) * mu_ref[...]
    r = ps[:, :RW_WIDTH]
    k = ps[:, RW_WIDTH:2 * RW_WIDTH]
    v = ps[:, 2 * RW_WIDTH:3 * RW_WIDTH]
    xw = ps[:, 3 * RW_WIDTH:3 * RW_WIDTH + W_LORA]
    xa = ps[:, 3 * RW_WIDTH + W_LORA:]

    t = -(w0_ref[...] + jnp.dot(jnp.tanh(xw).astype(BF16), w2_ref[...], preferred_element_type=F32))
    softplus = jnp.maximum(t, 0.0) + jnp.log1p(jnp.exp(-jnp.abs(t)))
    w_log = -softplus - 0.5
    lw_ref[...] = -jnp.exp(w_log)
    a_pre = a0_ref[...] + jnp.dot(xa.astype(BF16), a2_ref[...], preferred_element_type=F32)
    a = 1.0 / (1.0 + jnp.exp(-a_pre))
    kk = k * kkw_ref[...]
    sq = kk * kk
    sq_hi = sq.astype(BF16)
    sq_lo = (sq - sq_hi.astype(F32)).astype(BF16)
    ss = (jnp.dot(sq_hi, ones_ref[...], preferred_element_type=F32)
          + jnp.dot(sq_lo, ones_ref[...], preferred_element_type=F32))
    kk = kk / jnp.maximum(jnp.sqrt(ss), 1e-12)
    r_ref[...] = r
    kp_ref[...] = k * (1.0 + (a - 1.0) * ka_ref[...])
    vr_ref[...] = v
    kk_ref[...] = kk
    b_ref[...] = kk * a


def _attn_kernel(q_ref, k_ref, v_ref, o_ref):
    tq = TQ
    seq = q_ref.shape[1]
    row = lax.broadcasted_iota(jnp.int32, (tq, tq), 0)
    col = lax.broadcasted_iota(jnp.int32, (tq, tq), 1)
    causal = row >= col
    for i in range(seq // tq):
        e = (i + 1) * tq
        q = q_ref[0, i * tq:e, :]
        blocks = [lax.dot_general(q, k_ref[0, j * tq:(j + 1) * tq, :], (((1,), (1,)), ((), ())),
                                  preferred_element_type=F32) for j in range(i + 1)]
        blocks[i] = jnp.where(causal, blocks[i], -jnp.inf)
        m = blocks[0]
        for sb in blocks[1:]:
            m = jnp.maximum(m, sb)
        m = jnp.max(m, axis=-1, keepdims=True)
        p = [jnp.exp2(sb - m) for sb in blocks]
        l = p[0]
        for pb in p[1:]:
            l = l + pb
        l = jnp.sum(l, axis=-1, keepdims=True)
        pcat = jnp.concatenate([pb.astype(BF16) for pb in p], axis=-1)
        o = jnp.dot(pcat, v_ref[0, :e, :], preferred_element_type=F32)
        o_ref[i * tq:e, :] = o / l


def _bdot(a, b):
    return jnp.dot(a, b, preferred_element_type=F32)


def _bdot_nt(a, b):
    return lax.dot_general(a, b, (((1,), (1,)), ((), ())), preferred_element_type=F32)


def _wkv_kernel(r_ref, lw_ref, k_ref, v_ref, kk_ref, b_ref, tri_ref, rk_ref, g_ref, bb_ref,
                y_ref, h_ref):
    c = CHUNK
    n = RW_HEAD
    tm = r_ref.shape[0]
    nc = tm // c
    probs = [(ci, h) for ci in range(nc) for h in range(RW_HEADS)]
    bf = lambda x: x.astype(BF16)

    @pl.when(pl.program_id(1) == 0)
    def _():
        h_ref[...] = jnp.zeros(h_ref.shape, F32)

    lw = lw_ref[...]
    lw1 = lw.astype(BF16)
    rem = lw - lw1.astype(F32)
    lw2 = rem.astype(BF16)
    lw3 = (rem - lw2.astype(F32)).astype(BF16)
    tri = tri_ref[...]
    cum = _bdot(tri, lw1) + _bdot(tri, lw2) + _bdot(tri, lw3)

    r = r_ref[...]
    k = k_ref[...]
    v = v_ref[...]
    kk = kk_ref[...]
    b = b_ref[...]
    e_cum = jnp.exp(cum)
    e_neg = jnp.exp(-cum)
    r_t = r * e_cum
    kap_t = kk * jnp.exp(cum - lw)
    k_h = bf(k * e_neg)
    b_h = bf(b * e_neg)
    v_b = bf(v)
    bonus_w = r * k * rk_ref[...]
    x_cat, k_bar, b_bar, p_end = [], [], [], []
    for ci in range(nc):
        rows = slice(ci * c, (ci + 1) * c)
        cum_end = cum[(ci + 1) * c - 1:(ci + 1) * c, :]
        e_bar = jnp.exp(cum_end - cum[rows])
        k_bar.append(bf(k[rows] * e_bar))
        b_bar.append(bf(b[rows] * e_bar))
        p_end.append(jnp.exp(cum_end))
        x_cat.append(bf(jnp.concatenate([kap_t[rows], r_t[rows]], axis=0)))

    ii = lax.broadcasted_iota(jnp.int32, (c, c), 0)
    jj = lax.broadcasted_iota(jnp.int32, (c, c), 1)
    i2 = lax.broadcasted_iota(jnp.int32, (2 * c, c), 0)
    j2 = lax.broadcasted_iota(jnp.int32, (2 * c, c), 1)
    low2 = ((i2 < c) & (i2 > j2)) | (i2 - c >= j2)
    eye = (ii == jj).astype(F32)
    d8 = (ii // 8) == (jj // 8)
    offs = []
    blk = 8
    while blk < c:
        offs.append(((ii // (2 * blk)) == (jj // (2 * blk))) & ((ii // blk) != (jj // blk)))
        blk *= 2
    eye_n = (lax.broadcasted_iota(jnp.int32, (n, n), 0) == lax.broadcasted_iota(jnp.int32, (n, n), 1))

    def hsl(h):
        return slice(n * h, n * (h + 1))

    def rsl(ci):
        return slice(ci * c, (ci + 1) * c)

    a_b = {p: jnp.where(low2, _bdot_nt(x_cat[p[0]][:, hsl(p[1])], b_h[rsl(p[0]), hsl(p[1])]), 0.0) for p in probs}
    a_k = {p: bf(jnp.where(low2, _bdot_nt(x_cat[p[0]][:, hsl(p[1])], k_h[rsl(p[0]), hsl(p[1])]), 0.0)) for p in probs}
    a_kb = {p: a_b[p][:c] for p in probs}
    a_rb = {p: bf(a_b[p][c:]) for p in probs}

    d = {p: jnp.where(d8, a_kb[p], 0.0) for p in probs}
    db = {p: bf(d[p]) for p in probs}
    d2 = {p: _bdot(db[p], db[p]) for p in probs}
    d2b = {p: bf(d2[p]) for p in probs}
    d4 = {p: _bdot(d2b[p], d2b[p]) for p in probs}
    t = {p: _bdot(bf(eye - d[p]), bf(eye + d2[p])) for p in probs}
    t = {p: _bdot(bf(t[p]), bf(eye + d4[p])) for p in probs}
    for off in offs:
        tb = {p: bf(t[p]) for p in probs}
        ta = {p: _bdot(tb[p], bf(jnp.where(off, a_kb[p], 0.0))) for p in probs}
        t = {p: t[p] - _bdot(bf(ta[p]), tb[p]) for p in probs}
    tb = {p: bf(t[p]) for p in probs}

    akv = {p: _bdot(a_k[p], v_b[rsl(p[0]), hsl(p[1])]) for p in probs}
    w1 = {p: bf(_bdot(tb[p], x_cat[p[0]][:c, hsl(p[1])])) for p in probs}
    w2 = {p: bf(_bdot(tb[p], bf(akv[p][:c]))) for p in probs}
    r2 = {p: bf(r_t[rsl(p[0]), hsl(p[1])] - _bdot(a_rb[p], w1[p])) for p in probs}
    y2 = {p: akv[p][c:] - _bdot(a_rb[p], w2[p]) for p in probs}
    bbar_t = {p: b_bar[p[0]][:, hsl(p[1])].T for p in probs}
    kbar_t = {p: k_bar[p[0]][:, hsl(p[1])].T for p in probs}
    m_mat = {p: bf(jnp.where(eye_n, p_end[p[0]][:, hsl(p[1])], 0.0) - _bdot(bbar_t[p], w1[p])) for p in probs}
    g_mat = {p: _bdot(kbar_t[p], v_b[rsl(p[0]), hsl(p[1])]) - _bdot(bbar_t[p], w2[p]) for p in probs}

    hs = [h_ref[h] for h in range(RW_HEADS)]
    y = {}
    for ci in range(nc):
        hb = [bf(x) for x in hs]
        for h in range(RW_HEADS):
            y[(ci, h)] = _bdot(r2[(ci, h)], hb[h]) + y2[(ci, h)]
        hs = [_bdot(m_mat[(ci, h)], hb[h]) + g_mat[(ci, h)] for h in range(RW_HEADS)]
    for h in range(RW_HEADS):
        h_ref[h] = hs[h]

    for ci in range(nc):
        outs = []
        for h in range(RW_HEADS):
            yy = y[(ci, h)]
            mean = jnp.mean(yy, axis=-1, keepdims=True)
            yc = yy - mean
            var = jnp.mean(yc * yc, axis=-1, keepdims=True)
            yn = yc * lax.rsqrt(var + RW_GN_EPS) * g_ref[:, hsl(h)] + bb_ref[:, hsl(h)]
            bonus = jnp.sum(bonus_w[rsl(ci), hsl(h)], axis=-1, keepdims=True) * v[rsl(ci), hsl(h)]
            outs.append(yn + bonus)
        y_ref[rsl(ci), :] = jnp.concatenate(outs, axis=-1)


def _out_kernel(ym_ref, yr_ref, z_ref, x_ref, wo_ref, g_ref, o_ref):
    z = z_ref[...]
    gate = z / (1.0 + jnp.exp(-z))
    y = jnp.concatenate([ym_ref[...], yr_ref[...]], axis=-1) * gate
    out = jnp.dot(y.astype(BF16), wo_ref[...], preferred_element_type=F32)
    o_ref[...] = x_ref[...] + _rms(out, g_ref[...])


def _full(shape):
    nd = len(shape)
    return pl.BlockSpec(shape, lambda *_: (0,) * nd)


def _rot_cols(w):
    half = w.shape[-1] // 2
    return jnp.concatenate([-w[..., half:], w[..., :half]], axis=-1)


def kernel(x, positions, norm_pre_g, w_in, mla_q_norm_g, mla_w_uq, mla_kv_norm_g, mla_w_ukv,
           rw_mu, rw_w0, rw_w2, rw_a0, rw_a2, rw_k_k, rw_k_a, rw_r_k, rw_ln_g, rw_ln_b,
           w_out, norm_post_g):
    bsz, seq, _ = x.shape
    n_tok = bsz * seq
    assert norm_pre_g.shape[0] == 1
    assert seq % TM_PROJ == 0 and seq % TQ == 0 and seq % TM_WKV == 0 and n_tok % TM_OUT == 0
    row = lambda p: p.reshape(1, -1).astype(F32)

    x2 = x.reshape(n_tok, D_MODEL)
    pos = positions.reshape(n_tok, 1).astype(F32)
    inv_freq = ROPE_THETA ** (-jnp.arange(0, MLA_ROPE, 2, dtype=F32) / MLA_ROPE)
    invf = jnp.tile(inv_freq, 4).reshape(1, 128)

    w = w_in[0]
    w_kr = w[:, Q_LORA + KV_LORA:MLA_COLS]
    zeros64 = jnp.zeros((D_MODEL, 64), F32)
    wa = jnp.concatenate([w[:, :Q_LORA + KV_LORA], w_kr, zeros64, _rot_cols(w_kr), zeros64], axis=1).astype(BF16)
    wrw = w[:, MLA_COLS:MLA_COLS + RW_SHIFT_COLS].astype(BF16)
    wz = w[:, MLA_COLS + RW_SHIFT_COLS:].astype(BF16)
    wq = mla_w_uq[0].reshape(Q_LORA, MLA_HEADS, MLA_NOPE + MLA_ROPE)
    wqn = wq[:, :, :MLA_NOPE].reshape(Q_LORA, MLA_HEADS * 128).astype(BF16)
    wq_rope = wq[:, :, MLA_NOPE:]
    pad64 = jnp.zeros((Q_LORA, MLA_HEADS, 64), F32)
    wqr = jnp.concatenate([wq_rope, pad64], axis=-1).reshape(Q_LORA, MLA_HEADS * 128).astype(BF16)
    wqt = jnp.concatenate([_rot_cols(wq_rope), pad64], axis=-1).reshape(Q_LORA, MLA_HEADS * 128).astype(BF16)
    wkv = mla_w_ukv[0].reshape(KV_LORA, MLA_HEADS, MLA_NOPE + MLA_V)
    wkn = wkv[:, :, :MLA_NOPE].reshape(KV_LORA, MLA_HEADS * 128).astype(BF16)
    wkvv = wkv[:, :, MLA_NOPE:].reshape(KV_LORA, MLA_HEADS * 128).astype(BF16)
    head_id = np.arange(RW_WIDTH) // RW_HEAD
    ones_bd = jnp.asarray(head_id[:, None] == head_id[None, :], dtype=BF16)

    tm = TM_PROJ
    tok = lambda cols: pl.BlockSpec((tm, cols), lambda i: (i, 0))
    head_major = lambda cols: pl.BlockSpec((MLA_HEADS, tm, cols), lambda i: (0, i, 0))
    proj_in = [x2, pos, row(norm_pre_g), wa, wrw, wz,
               row(mla_q_norm_g), wqn, wqr, wqt, row(mla_kv_norm_g), wkn, wkvv, invf,
               row(rw_mu), row(rw_w0), rw_w2[0].astype(BF16), row(rw_a0), rw_a2[0].astype(BF16),
               row(rw_k_k), row(rw_k_a), ones_bd]
    proj_in_specs = [tok(D_MODEL), tok(1)] + [_full(a.shape) for a in proj_in[2:]]
    rw_shape = jax.ShapeDtypeStruct((n_tok, RW_WIDTH), F32)
    q, k, v, r, lw, kp, vr, kk, b, z = pl.pallas_call(
        functools.partial(_proj_kernel, seq // tm),
        grid=(n_tok // tm,),
        in_specs=proj_in_specs,
        out_specs=[head_major(QK_PAD), head_major(QK_PAD), head_major(MLA_V)]
                  + [tok(RW_WIDTH)] * 6 + [tok(D_MODEL)],
        out_shape=[jax.ShapeDtypeStruct((MLA_HEADS, n_tok, QK_PAD), BF16),
                   jax.ShapeDtypeStruct((MLA_HEADS, n_tok, QK_PAD), BF16),
                   jax.ShapeDtypeStruct((MLA_HEADS, n_tok, MLA_V), BF16)]
                  + [rw_shape] * 6 + [jax.ShapeDtypeStruct((n_tok, D_MODEL), F32)],
        scratch_shapes=[pltpu.VMEM((tm + 8, RW_SHIFT_COLS), F32)],
        compiler_params=pltpu.CompilerParams(dimension_semantics=("arbitrary",),
                                             vmem_limit_bytes=VMEM_LIMIT),
        name="proj",
    )(*proj_in)

    y_mla = pl.pallas_call(
        _attn_kernel,
        grid=(bsz, MLA_HEADS),
        in_specs=[pl.BlockSpec((1, seq, QK_PAD), lambda bi, h: (h, bi, 0)),
                  pl.BlockSpec((1, seq, QK_PAD), lambda bi, h: (h, bi, 0)),
                  pl.BlockSpec((1, seq, MLA_V), lambda bi, h: (h, bi, 0))],
        out_specs=pl.BlockSpec((seq, MLA_V), lambda bi, h: (bi, h)),
        out_shape=jax.ShapeDtypeStruct((n_tok, MLA_WIDTH), F32),
        compiler_params=pltpu.CompilerParams(dimension_semantics=("arbitrary",) * 2,
                                             vmem_limit_bytes=VMEM_LIMIT),
        name="attn",
    )(q, k, v)

    nt = seq // TM_WKV
    tok_id = np.arange(TM_WKV)
    tri = jnp.asarray((tok_id[:, None] >= tok_id[None, :]) & (tok_id[:, None] // CHUNK == tok_id[None, :] // CHUNK),
                      dtype=BF16)
    tile_spec = pl.BlockSpec((TM_WKV, RW_WIDTH), lambda bi, ti: (bi * nt + ti, 0))
    wkv_in = [r, lw, kp, vr, kk, b, tri, row(rw_r_k), row(rw_ln_g), row(rw_ln_b)]
    y_rw = pl.pallas_call(
        _wkv_kernel,
        grid=(bsz, nt),
        in_specs=[tile_spec] * 6 + [_full(a.shape) for a in wkv_in[6:]],
        out_specs=tile_spec,
        out_shape=rw_shape,
        scratch_shapes=[pltpu.VMEM((RW_HEADS, RW_HEAD, RW_HEAD), F32)],
        compiler_params=pltpu.CompilerParams(dimension_semantics=("arbitrary", "arbitrary"),
                                             vmem_limit_bytes=VMEM_LIMIT),
        name="wkv",
    )(*wkv_in)

    tmo = TM_OUT
    toko = lambda cols: pl.BlockSpec((tmo, cols), lambda i: (i, 0))
    out = pl.pallas_call(
        _out_kernel,
        grid=(n_tok // tmo,),
        in_specs=[toko(MLA_WIDTH), toko(RW_WIDTH), toko(D_MODEL), toko(D_MODEL),
                  _full((D_MODEL, D_MODEL)), _full((1, D_MODEL))],
        out_specs=toko(D_MODEL),
        out_shape=jax.ShapeDtypeStruct((n_tok, D_MODEL), F32),
        compiler_params=pltpu.CompilerParams(dimension_semantics=("arbitrary",),
                                             vmem_limit_bytes=VMEM_LIMIT),
        name="outproj",
    )(y_mla, y_rw, z, x2, w_out[0].astype(BF16), row(norm_post_g))
    return out.reshape(bsz, seq, D_MODEL)
```

```python
import functools

import jax
import jax.numpy as jnp
import numpy as np
from jax import lax
from jax.experimental import pallas as pl
from jax.experimental.pallas import tpu as pltpu

D_MODEL = 1024
MLA_HEADS = 4
MLA_NOPE = 128
MLA_ROPE = 64
MLA_V = 128
MLA_WIDTH = MLA_HEADS * MLA_V
Q_LORA = 256
KV_LORA = 128
ROPE_THETA = 10000.0
RW_HEAD = 64
RW_WIDTH = 512
RW_HEADS = 8
W_LORA = 64
A_LORA = 64
RW_GN_EPS = 64e-5
NORM_EPS = 1e-6
MLA_COLS = Q_LORA + KV_LORA + MLA_ROPE
RW_SHIFT_COLS = 3 * RW_WIDTH + W_LORA + A_LORA
QK_PAD = 256
WA_COLS = Q_LORA + KV_LORA + 128 + 128

CHUNK = 64
TM_WKV = 256
TM_PROJ = 512
TQ = 256
TM_OUT = 512
VMEM_LIMIT = 48 * 1024 * 1024

F32 = jnp.float32
BF16 = jnp.bfloat16


def _rms(x, g):
    return x * lax.rsqrt(jnp.mean(x * x, axis=-1, keepdims=True) + NORM_EPS) * g


def _proj_kernel(tiles_per_seq,
                 x_ref, pos_ref, gpre_ref, wa_ref, wrw_ref, wz_ref,
                 qg_ref, wqn_ref, wqr_ref, wqt_ref, kvg_ref, wkn_ref, wkv_ref, invf_ref,
                 mu_ref, w0_ref, w2_ref, a0_ref, a2_ref, kkw_ref, ka_ref, ones_ref,
                 q_ref, k_ref, v_ref, r_ref, lw_ref, kp_ref, vr_ref, kk_ref, b_ref, z_ref,
                 shift_ref):
    tm = x_ref.shape[0]
    i = pl.program_id(0)
    scale = float((MLA_NOPE + MLA_ROPE) ** -0.5 * np.log2(np.e))

    u = _rms(x_ref[...], gpre_ref[...]).astype(BF16)

    pa = jnp.dot(u, wa_ref[...], preferred_element_type=F32)
    c_q = pa[:, :Q_LORA]
    c_kv = pa[:, Q_LORA:Q_LORA + KV_LORA]
    kr = pa[:, Q_LORA + KV_LORA:Q_LORA + KV_LORA + 128]
    kr_rot = pa[:, Q_LORA + KV_LORA + 128:]

    ang = pos_ref[...] * invf_ref[...]
    cos = jnp.cos(ang)
    sin = jnp.sin(ang)

    cqn = _rms(c_q, qg_ref[...]).astype(BF16)
    qn = jnp.dot(cqn, wqn_ref[...], preferred_element_type=F32)
    qr = jnp.dot(cqn, wqr_ref[...], preferred_element_type=F32)
    qt = jnp.dot(cqn, wqt_ref[...], preferred_element_type=F32)
    ckn = _rms(c_kv, kvg_ref[...]).astype(BF16)
    kn = jnp.dot(ckn, wkn_ref[...], preferred_element_type=F32)
    vv = jnp.dot(ckn, wkv_ref[...], preferred_element_type=F32)
    k_rope = (kr * cos + kr_rot * sin).astype(BF16)
    for h in range(MLA_HEADS):
        sl = slice(128 * h, 128 * (h + 1))
        q_ref[h, :, :128] = (qn[:, sl] * scale).astype(BF16)
        q_ref[h, :, 128:] = ((qr[:, sl] * cos + qt[:, sl] * sin) * scale).astype(BF16)
        k_ref[h, :, :128] = kn[:, sl].astype(BF16)
        k_ref[h, :, 128:] = k_rope
        v_ref[h] = vv[:, sl].astype(BF16)

    z_ref[...] = jnp.dot(u, wz_ref[...], preferred_element_type=F32)

    prw = jnp.dot(u, wrw_ref[...], preferred_element_type=F32)

    @pl.when(i % tiles_per_seq == 0)
    def _():
        shift_ref[0:8, :] = jnp.zeros((8, RW_SHIFT_COLS), F32)

    @pl.when(i % tiles_per_seq != 0)
    def _():
        shift_ref[7:8, :] = shift_ref[tm + 7:tm + 8, :]

    shift_ref[8:tm + 8, :] = prw
    prev = shift_ref[7:tm + 7, :]
    ps = prw + (prev - prw) * mu_ref[...]
    r = ps[:, :RW_WIDTH]
    k = ps[:, RW_WIDTH:2 * RW_WIDTH]
    v = ps[:, 2 * RW_WIDTH:3 * RW_WIDTH]
    xw = ps[:, 3 * RW_WIDTH:3 * RW_WIDTH + W_LORA]
    xa = ps[:, 3 * RW_WIDTH + W_LORA:]

    t = -(w0_ref[...] + jnp.dot(jnp.tanh(xw).astype(BF16), w2_ref[...], preferred_element_type=F32))
    softplus = jnp.maximum(t, 0.0) + jnp.log1p(jnp.exp(-jnp.abs(t)))
    w_log = -softplus - 0.5
    lw_ref[...] = -jnp.exp(w_log)
    a_pre = a0_ref[...] + jnp.dot(xa.astype(BF16), a2_ref[...], preferred_element_type=F32)
    a = 1.0 / (1.0 + jnp.exp(-a_pre))
    kk = k * kkw_ref[...]
    sq = kk * kk
    sq_hi = sq.astype(BF16)
    sq_lo = (sq - sq_hi.astype(F32)).astype(BF16)
    ss = (jnp.dot(sq_hi, ones_ref[...], preferred_element_type=F32)
          + jnp.dot(sq_lo, ones_ref[...], preferred_element_type=F32))
    kk = kk / jnp.maximum(jnp.sqrt(ss), 1e-12)
    r_ref[...] = r
    kp_ref[...] = k * (1.0 + (a - 1.0) * ka_ref[...])
    vr_ref[...] = v
    kk_ref[...] = kk
    b_ref[...] = kk * a


def _attn_kernel(q_ref, k_ref, v_ref, o_ref):
    tq = TQ
    seq = q_ref.shape[1]
    row = lax.broadcasted_iota(jnp.int32, (tq, tq), 0)
    col = lax.broadcasted_iota(jnp.int32, (tq, tq), 1)
    causal = row >= col
    for i in range(seq // tq):
        e = (i + 1) * tq
        q = q_ref[0, i * tq:e, :]
        blocks = [lax.dot_general(q, k_ref[0, j * tq:(j + 1) * tq, :], (((1,), (1,)), ((), ())),
                                  preferred_element_type=F32) for j in range(i + 1)]
        blocks[i] = jnp.where(causal, blocks[i], -jnp.inf)
        m = blocks[0]
        for sb in blocks[1:]:
            m = jnp.maximum(m, sb)
        m = jnp.max(m, axis=-1, keepdims=True)
        p = [jnp.exp2(sb - m) for sb in blocks]
        l = p[0]
        for pb in p[1:]:
            l = l + pb
        l = jnp.sum(l, axis=-1, keepdims=True)
        pcat = jnp.concatenate([pb.astype(BF16) for pb in p], axis=-1)
        o = jnp.dot(pcat, v_ref[0, :e, :], preferred_element_type=F32)
        o_ref[i * tq:e, :] = o / l


def _bdot(a, b):
    return jnp.dot(a, b, preferred_element_type=F32)


def _bdot_nt(a, b):
    return lax.dot_general(a, b, (((1,), (1,)), ((), ())), preferred_element_type=F32)


def _wkv_kernel(r_ref, lw_ref, k_ref, v_ref, kk_ref, b_ref, tri_ref, rk_ref, g_ref, bb_ref,
                y_ref, h_ref):
    c = CHUNK
    tm = r_ref.shape[0]
    nc = tm // c
    npair = RW_WIDTH // 128
    probs = [(ci, q) for ci in range(nc) for q in range(npair)]
    bf = lambda x: x.astype(BF16)

    @pl.when(pl.program_id(1) == 0)
    def _():
        h_ref[...] = jnp.zeros(h_ref.shape, F32)

    lw = lw_ref[...]
    lw1 = lw.astype(BF16)
    rem = lw - lw1.astype(F32)
    lw2 = rem.astype(BF16)
    lw3 = (rem - lw2.astype(F32)).astype(BF16)
    tri = tri_ref[...]
    cum = _bdot(tri, lw1) + _bdot(tri, lw2) + _bdot(tri, lw3)

    r = r_ref[...]
    k = k_ref[...]
    v = v_ref[...]
    kk = kk_ref[...]
    b = b_ref[...]
    e_neg = jnp.exp(-cum)
    r_t = r * jnp.exp(cum)
    kap_t = kk * jnp.exp(cum - lw)
    k_h = k * e_neg
    b_h = b * e_neg
    bonus_w = r * k * rk_ref[...]
    zbar, p_end = [], []
    for ci in range(nc):
        rows = slice(ci * c, (ci + 1) * c)
        cum_end = cum[(ci + 1) * c - 1:(ci + 1) * c, :]
        e_bar = jnp.exp(cum_end - cum[rows])
        zbar.append(jnp.concatenate([k[rows] * e_bar, -(b[rows] * e_bar)], axis=0))
        p_end.append(jnp.exp(cum_end))

    i1 = lax.broadcasted_iota(jnp.int32, (c, 128), 0)
    j1 = lax.broadcasted_iota(jnp.int32, (c, 128), 1) % c
    i2 = lax.broadcasted_iota(jnp.int32, (2 * c, 128), 0)
    l2 = lax.broadcasted_iota(jnp.int32, (2 * c, 128), 1)
    j2 = l2 % c
    bd = (i2 // c) == (l2 // c)
    low2 = ((i2 < c) & (i2 > j2)) | (i2 - c >= j2)
    first = j1 == lax.broadcasted_iota(jnp.int32, (c, 128), 1)
    eye = (i1 == j1).astype(F32)
    d8 = (i1 // 8) == (j1 // 8)
    offs = []
    blk = 8
    while blk < c:
        offs.append(((i1 // (2 * blk)) == (j1 // (2 * blk))) & ((i1 // blk) != (j1 // blk)))
        blk *= 2

    def psl(q):
        return slice(128 * q, 128 * (q + 1))

    def rsl(ci):
        return slice(ci * c, (ci + 1) * c)

    def blockdiag(x):
        return bf(jnp.where(bd, jnp.concatenate([x, x], axis=0), 0.0))

    def pick(x):
        return jnp.where(first, x[:c], x[c:])

    sub = lambda a, p: a[rsl(p[0]), psl(p[1])]

    x_cat = {p: bf(jnp.concatenate([sub(kap_t, p), sub(r_t, p)], axis=0)) for p in probs}
    a_b = {p: jnp.where(low2, _bdot_nt(x_cat[p], blockdiag(sub(b_h, p))), 0.0) for p in probs}
    a_k = {p: bf(jnp.where(low2, _bdot_nt(x_cat[p], blockdiag(sub(k_h, p))), 0.0)) for p in probs}
    a_kb = {p: a_b[p][:c] for p in probs}
    a_rb = {p: bf(a_b[p][c:]) for p in probs}

    d = {p: jnp.where(d8, a_kb[p], 0.0) for p in probs}
    d2 = {p: _bdot(bf(d[p]), blockdiag(d[p])) for p in probs}
    d4 = {p: _bdot(bf(d2[p]), blockdiag(d2[p])) for p in probs}
    t = {p: _bdot(bf(eye - d[p]), blockdiag(eye + d2[p])) for p in probs}
    t = {p: _bdot(bf(t[p]), blockdiag(eye + d4[p])) for p in probs}
    for off in offs:
        ta = {p: _bdot(bf(t[p]), blockdiag(jnp.where(off, a_kb[p], 0.0))) for p in probs}
        t = {p: t[p] - _bdot(bf(ta[p]), blockdiag(t[p])) for p in probs}
    tb = {p: bf(t[p]) for p in probs}

    vbd = {p: blockdiag(sub(v, p)) for p in probs}
    akv = {p: _bdot(a_k[p], vbd[p]) for p in probs}
    w1 = {p: _bdot(tb[p], blockdiag(sub(kap_t, p))) for p in probs}
    w2 = {p: _bdot(tb[p], blockdiag(akv[p][:c])) for p in probs}
    r2 = {p: sub(r_t, p) - _bdot(a_rb[p], blockdiag(w1[p])) for p in probs}
    y2 = {p: akv[p][c:] - _bdot(a_rb[p], blockdiag(w2[p])) for p in probs}
    zt = {p: bf(zbar[p[0]][:, psl(p[1])].T) for p in probs}
    g_mat = {p: pick(_bdot(zt[p], bf(jnp.concatenate([sub(v, p), w2[p]], axis=0)))) for p in probs}
    m_mat = {p: jnp.where(eye > 0, p_end[p[0]][:, psl(p[1])], 0.0)
             + pick(_bdot(zt[p], bf(jnp.concatenate([jnp.zeros_like(w1[p]), w1[p]], axis=0)))) for p in probs}
    rm = {p: bf(jnp.concatenate([r2[p], m_mat[p]], axis=0)) for p in probs}

    hs = [h_ref[q] for q in range(npair)]
    for ci in range(nc):
        for q in range(npair):
            p = (ci, q)
            yh = _bdot(rm[p], blockdiag(hs[q]))
            hs[q] = yh[c:] + g_mat[p]
            yy = yh[:c] + y2[p]

            def seg_mean(x):
                s0 = jnp.sum(jnp.where(first, x, 0.0), axis=-1, keepdims=True)
                s1 = jnp.sum(jnp.where(first, 0.0, x), axis=-1, keepdims=True)
                return jnp.where(first, s0, s1) * (1.0 / c)

            yc = yy - seg_mean(yy)
            var = seg_mean(yc * yc)
            yn = yc * lax.rsqrt(var + RW_GN_EPS) * g_ref[:, psl(q)] + bb_ref[:, psl(q)]
            bonus = seg_mean(sub(bonus_w, p)) * float(c) * sub(v, p)
            y_ref[rsl(ci), psl(q)] = yn + bonus
    for q in range(npair):
        h_ref[q] = hs[q]


def _out_kernel(ym_ref, yr_ref, z_ref, x_ref, wo_ref, g_ref, o_ref):
    z = z_ref[...]
    gate = z / (1.0 + jnp.exp(-z))
    y = jnp.concatenate([ym_ref[...], yr_ref[...]], axis=-1) * gate
    out = jnp.dot(y.astype(BF16), wo_ref[...], preferred_element_type=F32)
    o_ref[...] = x_ref[...] + _rms(out, g_ref[...])


def _full(shape):
    nd = len(shape)
    return pl.BlockSpec(shape, lambda *_: (0,) * nd)


def _rot_cols(w):
    half = w.shape[-1] // 2
    return jnp.concatenate([-w[..., half:], w[..., :half]], axis=-1)


def kernel(x, positions, norm_pre_g, w_in, mla_q_norm_g, mla_w_uq, mla_kv_norm_g, mla_w_ukv,
           rw_mu, rw_w0, rw_w2, rw_a0, rw_a2, rw_k_k, rw_k_a, rw_r_k, rw_ln_g, rw_ln_b,
           w_out, norm_post_g):
    bsz, seq, _ = x.shape
    n_tok = bsz * seq
    assert norm_pre_g.shape[0] == 1
    assert seq % TM_PROJ == 0 and seq % TQ == 0 and seq % TM_WKV == 0 and n_tok % TM_OUT == 0
    row = lambda p: p.reshape(1, -1).astype(F32)

    x2 = x.reshape(n_tok, D_MODEL)
    pos = positions.reshape(n_tok, 1).astype(F32)
    inv_freq = ROPE_THETA ** (-jnp.arange(0, MLA_ROPE, 2, dtype=F32) / MLA_ROPE)
    invf = jnp.tile(inv_freq, 4).reshape(1, 128)

    w = w_in[0]
    w_kr = w[:, Q_LORA + KV_LORA:MLA_COLS]
    zeros64 = jnp.zeros((D_MODEL, 64), F32)
    wa = jnp.concatenate([w[:, :Q_LORA + KV_LORA], w_kr, zeros64, _rot_cols(w_kr), zeros64], axis=1).astype(BF16)
    wrw = w[:, MLA_COLS:MLA_COLS + RW_SHIFT_COLS].astype(BF16)
    wz = w[:, MLA_COLS + RW_SHIFT_COLS:].astype(BF16)
    wq = mla_w_uq[0].reshape(Q_LORA, MLA_HEADS, MLA_NOPE + MLA_ROPE)
    wqn = wq[:, :, :MLA_NOPE].reshape(Q_LORA, MLA_HEADS * 128).astype(BF16)
    wq_rope = wq[:, :, MLA_NOPE:]
    pad64 = jnp.zeros((Q_LORA, MLA_HEADS, 64), F32)
    wqr = jnp.concatenate([wq_rope, pad64], axis=-1).reshape(Q_LORA, MLA_HEADS * 128).astype(BF16)
    wqt = jnp.concatenate([_rot_cols(wq_rope), pad64], axis=-1).reshape(Q_LORA, MLA_HEADS * 128).astype(BF16)
    wkv = mla_w_ukv[0].reshape(KV_LORA, MLA_HEADS, MLA_NOPE + MLA_V)
    wkn = wkv[:, :, :MLA_NOPE].reshape(KV_LORA, MLA_HEADS * 128).astype(BF16)
    wkvv = wkv[:, :, MLA_NOPE:].reshape(KV_LORA, MLA_HEADS * 128).astype(BF16)
    head_id = np.arange(RW_WIDTH) // RW_HEAD
    ones_bd = jnp.asarray(head_id[:, None] == head_id[None, :], dtype=BF16)

    tm = TM_PROJ
    tok = lambda cols: pl.BlockSpec((tm, cols), lambda i: (i, 0))
    head_major = lambda cols: pl.BlockSpec((MLA_HEADS, tm, cols), lambda i: (0, i, 0))
    proj_in = [x2, pos, row(norm_pre_g), wa, wrw, wz,
               row(mla_q_norm_g), wqn, wqr, wqt, row(mla_kv_norm_g), wkn, wkvv, invf,
               row(rw_mu), row(rw_w0), rw_w2[0].astype(BF16), row(rw_a0), rw_a2[0].astype(BF16),
               row(rw_k_k), row(rw_k_a), ones_bd]
    proj_in_specs = [tok(D_MODEL), tok(1)] + [_full(a.shape) for a in proj_in[2:]]
    rw_shape = jax.ShapeDtypeStruct((n_tok, RW_WIDTH), F32)
    q, k, v, r, lw, kp, vr, kk, b, z = pl.pallas_call(
        functools.partial(_proj_kernel, seq // tm),
        grid=(n_tok // tm,),
        in_specs=proj_in_specs,
        out_specs=[head_major(QK_PAD), head_major(QK_PAD), head_major(MLA_V)]
                  + [tok(RW_WIDTH)] * 6 + [tok(D_MODEL)],
        out_shape=[jax.ShapeDtypeStruct((MLA_HEADS, n_tok, QK_PAD), BF16),
                   jax.ShapeDtypeStruct((MLA_HEADS, n_tok, QK_PAD), BF16),
                   jax.ShapeDtypeStruct((MLA_HEADS, n_tok, MLA_V), BF16)]
                  + [rw_shape] * 6 + [jax.ShapeDtypeStruct((n_tok, D_MODEL), F32)],
        scratch_shapes=[pltpu.VMEM((tm + 8, RW_SHIFT_COLS), F32)],
        compiler_params=pltpu.CompilerParams(dimension_semantics=("arbitrary",),
                                             vmem_limit_bytes=VMEM_LIMIT),
        name="proj",
    )(*proj_in)

    y_mla = pl.pallas_call(
        _attn_kernel,
        grid=(bsz, MLA_HEADS),
        in_specs=[pl.BlockSpec((1, seq, QK_PAD), lambda bi, h: (h, bi, 0)),
                  pl.BlockSpec((1, seq, QK_PAD), lambda bi, h: (h, bi, 0)),
                  pl.BlockSpec((1, seq, MLA_V), lambda bi, h: (h, bi, 0))],
        out_specs=pl.BlockSpec((seq, MLA_V), lambda bi, h: (bi, h)),
        out_shape=jax.ShapeDtypeStruct((n_tok, MLA_WIDTH), F32),
        compiler_params=pltpu.CompilerParams(dimension_semantics=("arbitrary",) * 2,
                                             vmem_limit_bytes=VMEM_LIMIT),
        name="attn",
    )(q, k, v)

    nt = seq // TM_WKV
    tok_id = np.arange(TM_WKV)
    tri = jnp.asarray((tok_id[:, None] >= tok_id[None, :]) & (tok_id[:, None] // CHUNK == tok_id[None, :] // CHUNK),
                      dtype=BF16)
    tile_spec = pl.BlockSpec((TM_WKV, RW_WIDTH), lambda bi, ti: (bi * nt + ti, 0))
    wkv_in = [r, lw, kp, vr, kk, b, tri, row(rw_r_k), row(rw_ln_g), row(rw_ln_b)]
    y_rw = pl.pallas_call(
        _wkv_kernel,
        grid=(bsz, nt),
        in_specs=[tile_spec] * 6 + [_full(a.shape) for a in wkv_in[6:]],
        out_specs=tile_spec,
        out_shape=rw_shape,
        scratch_shapes=[pltpu.VMEM((RW_WIDTH // 128, RW_HEAD, 128), F32)],
        compiler_params=pltpu.CompilerParams(dimension_semantics=("arbitrary", "arbitrary"),
                                             vmem_limit_bytes=VMEM_LIMIT),
        name="wkv",
    )(*wkv_in)

    tmo = TM_OUT
    toko = lambda cols: pl.BlockSpec((tmo, cols), lambda i: (i, 0))
    out = pl.pallas_call(
        _out_kernel,
        grid=(n_tok // tmo,),
        in_specs=[toko(MLA_WIDTH), toko(RW_WIDTH), toko(D_MODEL), toko(D_MODEL),
                  _full((D_MODEL, D_MODEL)), _full((1, D_MODEL))],
        out_specs=toko(D_MODEL),
        out_shape=jax.ShapeDtypeStruct((n_tok, D_MODEL), F32),
        compiler_params=pltpu.CompilerParams(dimension_semantics=("arbitrary",),
                                             vmem_limit_bytes=VMEM_LIMIT),
        name="outproj",
    )(y_mla, y_rw, z, x2, w_out[0].astype(BF16), row(norm_post_g))
    return out.reshape(bsz, seq, D_MODEL)
```

```python
import functools

import jax
import jax.numpy as jnp
import numpy as np
from jax import lax
from jax.experimental import pallas as pl
from jax.experimental.pallas import tpu as pltpu

D_MODEL = 1024
MLA_HEADS = 4
MLA_NOPE = 128
MLA_ROPE = 64
MLA_V = 128
MLA_WIDTH = MLA_HEADS * MLA_V
Q_LORA = 256
KV_LORA = 128
ROPE_THETA = 10000.0
RW_HEAD = 64
RW_WIDTH = 512
RW_HEADS = 8
W_LORA = 64
A_LORA = 64
RW_GN_EPS = 64e-5
NORM_EPS = 1e-6
MLA_COLS = Q_LORA + KV_LORA + MLA_ROPE
RW_SHIFT_COLS = 3 * RW_WIDTH + W_LORA + A_LORA
QK_PAD = 256
WA_COLS = Q_LORA + KV_LORA + 128 + 128

CHUNK = 64
TM_WKV = 256
SEQ_WKV = 2
TM_PROJ = 512
TQ = 256
TM_OUT = 512
VMEM_LIMIT = 48 * 1024 * 1024

F32 = jnp.float32
BF16 = jnp.bfloat16


def _rms(x, g):
    return x * lax.rsqrt(jnp.mean(x * x, axis=-1, keepdims=True) + NORM_EPS) * g


def _proj_kernel(tiles_per_seq,
                 x_ref, pos_ref, gpre_ref, wa_ref, wrw_ref, wz_ref,
                 qg_ref, wqn_ref, wqr_ref, wqt_ref, kvg_ref, wkn_ref, wkv_ref, invf_ref,
                 mu_ref, w0_ref, w2_ref, a0_ref, a2_ref, kkw_ref, ka_ref, ones_ref,
                 q_ref, k_ref, v_ref, r_ref, lw_ref, kp_ref, vr_ref, kk_ref, b_ref, z_ref,
                 shift_ref):
    tm = x_ref.shape[0]
    i = pl.program_id(0)
    scale = float((MLA_NOPE + MLA_ROPE) ** -0.5 * np.log2(np.e))

    u = _rms(x_ref[...], gpre_ref[...]).astype(BF16)

    pa = jnp.dot(u, wa_ref[...], preferred_element_type=F32)
    c_q = pa[:, :Q_LORA]
    c_kv = pa[:, Q_LORA:Q_LORA + KV_LORA]
    kr = pa[:, Q_LORA + KV_LORA:Q_LORA + KV_LORA + 128]
    kr_rot = pa[:, Q_LORA + KV_LORA + 128:]

    ang = invf_ref[...] * pos_ref[0]
    cos_t = jnp.cos(ang)
    sin_t = jnp.sin(ang)
    cos = jnp.concatenate([cos_t] * 4, axis=0).T
    sin = jnp.concatenate([sin_t] * 4, axis=0).T

    cqn = _rms(c_q, qg_ref[...]).astype(BF16)
    qn = jnp.dot(cqn, wqn_ref[...], preferred_element_type=F32)
    qr = jnp.dot(cqn, wqr_ref[...], preferred_element_type=F32)
    qt = jnp.dot(cqn, wqt_ref[...], preferred_element_type=F32)
    ckn = _rms(c_kv, kvg_ref[...]).astype(BF16)
    kn = jnp.dot(ckn, wkn_ref[...], preferred_element_type=F32)
    vv = jnp.dot(ckn, wkv_ref[...], preferred_element_type=F32)
    k_rope = (kr * cos + kr_rot * sin).astype(BF16)
    for h in range(MLA_HEADS):
        sl = slice(128 * h, 128 * (h + 1))
        q_ref[h, :, :128] = (qn[:, sl] * scale).astype(BF16)
        q_ref[h, :, 128:] = ((qr[:, sl] * cos + qt[:, sl] * sin) * scale).astype(BF16)
        k_ref[h, :, :128] = kn[:, sl].astype(BF16)
        k_ref[h, :, 128:] = k_rope
        v_ref[h] = vv[:, sl].astype(BF16)

    z = jnp.dot(u, wz_ref[...], preferred_element_type=F32)
    z_ref[...] = (z / (1.0 + jnp.exp(-z))).astype(BF16)

    prw = jnp.dot(u, wrw_ref[...], preferred_element_type=F32)

    @pl.when(i % tiles_per_seq == 0)
    def _():
        shift_ref[0:8, :] = jnp.zeros((8, RW_SHIFT_COLS), F32)

    @pl.when(i % tiles_per_seq != 0)
    def _():
        shift_ref[7:8, :] = shift_ref[tm + 7:tm + 8, :]

    shift_ref[8:tm + 8, :] = prw
    prev = shift_ref[7:tm + 7, :]
    ps = prw + (prev - prw) * mu_ref[...]
    r = ps[:, :RW_WIDTH]
    k = ps[:, RW_WIDTH:2 * RW_WIDTH]
    v = ps[:, 2 * RW_WIDTH:3 * RW_WIDTH]
    xw = ps[:, 3 * RW_WIDTH:3 * RW_WIDTH + W_LORA]
    xa = ps[:, 3 * RW_WIDTH + W_LORA:]

    t = -(w0_ref[...] + jnp.dot(jnp.tanh(xw).astype(BF16), w2_ref[...], preferred_element_type=F32))
    softplus = jnp.maximum(t, 0.0) + jnp.log1p(jnp.exp(-jnp.abs(t)))
    w_log = -softplus - 0.5
    lw_ref[...] = -jnp.exp(w_log)
    a_pre = a0_ref[...] + jnp.dot(xa.astype(BF16), a2_ref[...], preferred_element_type=F32)
    a = 1.0 / (1.0 + jnp.exp(-a_pre))
    kk = k * kkw_ref[...]
    sq = kk * kk
    ss = jnp.dot(sq.astype(BF16), ones_ref[...], preferred_element_type=F32)
    kk = kk / jnp.maximum(jnp.sqrt(ss), 1e-12)
    r_ref[...] = r
    kp_ref[...] = k * (1.0 + (a - 1.0) * ka_ref[...])
    vr_ref[...] = v
    kk_ref[...] = kk
    b_ref[...] = kk * a


def _attn_kernel(q_ref, k_ref, v_ref, o_ref):
    tq = TQ
    seq = q_ref.shape[1]
    row = lax.broadcasted_iota(jnp.int32, (tq, tq), 0)
    col = lax.broadcasted_iota(jnp.int32, (tq, tq), 1)
    causal = row >= col
    for i in range(seq // tq):
        e = (i + 1) * tq
        q = q_ref[0, i * tq:e, :]
        blocks = [lax.dot_general(q, k_ref[0, j * tq:(j + 1) * tq, :], (((1,), (1,)), ((), ())),
                                  preferred_element_type=F32) for j in range(i + 1)]
        blocks[i] = jnp.where(causal, blocks[i], -jnp.inf)
        m = blocks[0]
        for sb in blocks[1:]:
            m = jnp.maximum(m, sb)
        m = jnp.max(m, axis=-1, keepdims=True)
        p = [jnp.exp2(sb - m) for sb in blocks]
        l = p[0]
        for pb in p[1:]:
            l = l + pb
        l = jnp.sum(l, axis=-1, keepdims=True)
        pcat = jnp.concatenate([pb.astype(BF16) for pb in p], axis=-1)
        o = jnp.dot(pcat, v_ref[0, :e, :], preferred_element_type=F32)
        o_ref[i * tq:e, :] = (o / l).astype(o_ref.dtype)


def _bdot(a, b):
    return jnp.dot(a, b, preferred_element_type=F32)


def _bdot_nt(a, b):
    return lax.dot_general(a, b, (((1,), (1,)), ((), ())), preferred_element_type=F32)


def _wkv_kernel(r_ref, lw_ref, k_ref, v_ref, kk_ref, b_ref, tri_ref, rk_ref, g_ref, bb_ref,
                y_ref, h_ref):
    c = CHUNK
    nseq, tm = r_ref.shape[0], r_ref.shape[1]
    nc = tm // c
    npair = RW_WIDTH // 128
    probs = [(s, ci, q) for s in range(nseq) for ci in range(nc) for q in range(npair)]
    bf = lambda x: x.astype(BF16)

    @pl.when(pl.program_id(1) == 0)
    def _():
        h_ref[...] = jnp.zeros(h_ref.shape, F32)

    tri = tri_ref[...]
    v, r_t, kap_t, k_h, b_h, bonus_w, zbar, p_end = [], [], [], [], [], [], [], []
    for s in range(nseq):
        lw = lw_ref[s]
        lw1 = lw.astype(BF16)
        rem = lw - lw1.astype(F32)
        lw2 = rem.astype(BF16)
        lw3 = (rem - lw2.astype(F32)).astype(BF16)
        cum = _bdot(tri, lw1) + _bdot(tri, lw2) + _bdot(tri, lw3)
        r = r_ref[s]
        k = k_ref[s]
        b = b_ref[s]
        e_neg = jnp.exp(-cum)
        v.append(v_ref[s])
        r_t.append(r * jnp.exp(cum))
        kap_t.append(kk_ref[s] * jnp.exp(cum - lw))
        k_h.append(k * e_neg)
        b_h.append(b * e_neg)
        bonus_w.append(r * k * rk_ref[...])
        zb, pe = [], []
        for ci in range(nc):
            rows = slice(ci * c, (ci + 1) * c)
            cum_end = cum[(ci + 1) * c - 1:(ci + 1) * c, :]
            e_bar = jnp.exp(cum_end - cum[rows])
            zb.append(jnp.concatenate([k[rows] * e_bar, -(b[rows] * e_bar)], axis=0))
            pe.append(jnp.exp(cum_end))
        zbar.append(zb)
        p_end.append(pe)

    i1 = lax.broadcasted_iota(jnp.int32, (c, 128), 0)
    j1 = lax.broadcasted_iota(jnp.int32, (c, 128), 1) % c
    i2 = lax.broadcasted_iota(jnp.int32, (2 * c, 128), 0)
    l2 = lax.broadcasted_iota(jnp.int32, (2 * c, 128), 1)
    j2 = l2 % c
    bd = (i2 // c) == (l2 // c)
    low2 = ((i2 < c) & (i2 > j2)) | (i2 - c >= j2)
    first = j1 == lax.broadcasted_iota(jnp.int32, (c, 128), 1)
    eye = (i1 == j1).astype(F32)
    d8 = (i1 // 8) == (j1 // 8)
    offs = []
    blk = 8
    while blk < c:
        offs.append(((i1 // (2 * blk)) == (j1 // (2 * blk))) & ((i1 // blk) != (j1 // blk)))
        blk *= 2

    def psl(q):
        return slice(128 * q, 128 * (q + 1))

    def rsl(ci):
        return slice(ci * c, (ci + 1) * c)

    def blockdiag(x):
        return bf(jnp.where(bd, jnp.concatenate([x, x], axis=0), 0.0))

    def pick(x):
        return jnp.where(first, x[:c], x[c:])

    sub = lambda a, p: a[p[0]][rsl(p[1]), psl(p[2])]

    x_cat = {p: bf(jnp.concatenate([sub(kap_t, p), sub(r_t, p)], axis=0)) for p in probs}
    a_b = {p: jnp.where(low2, _bdot_nt(x_cat[p], blockdiag(sub(b_h, p))), 0.0) for p in probs}
    a_k = {p: bf(jnp.where(low2, _bdot_nt(x_cat[p], blockdiag(sub(k_h, p))), 0.0)) for p in probs}
    a_kb = {p: a_b[p][:c] for p in probs}
    a_rb = {p: bf(a_b[p][c:]) for p in probs}

    d = {p: jnp.where(d8, a_kb[p], 0.0) for p in probs}
    d2 = {p: _bdot(bf(d[p]), blockdiag(d[p])) for p in probs}
    d4 = {p: _bdot(bf(d2[p]), blockdiag(d2[p])) for p in probs}
    t = {p: _bdot(bf(eye - d[p]), blockdiag(eye + d2[p])) for p in probs}
    t = {p: _bdot(bf(t[p]), blockdiag(eye + d4[p])) for p in probs}
    for off in offs:
        ta = {p: _bdot(bf(t[p]), blockdiag(jnp.where(off, a_kb[p], 0.0))) for p in probs}
        t = {p: t[p] - _bdot(bf(ta[p]), blockdiag(t[p])) for p in probs}
    tb = {p: bf(t[p]) for p in probs}

    vbd = {p: blockdiag(sub(v, p)) for p in probs}
    akv = {p: _bdot(a_k[p], vbd[p]) for p in probs}
    w1 = {p: _bdot(tb[p], blockdiag(sub(kap_t, p))) for p in probs}
    w2 = {p: _bdot(tb[p], blockdiag(akv[p][:c])) for p in probs}
    r2 = {p: sub(r_t, p) - _bdot(a_rb[p], blockdiag(w1[p])) for p in probs}
    y2 = {p: akv[p][c:] - _bdot(a_rb[p], blockdiag(w2[p])) for p in probs}
    zt = {p: bf(zbar[p[0]][p[1]][:, psl(p[2])].T) for p in probs}
    g_mat = {p: pick(_bdot(zt[p], bf(jnp.concatenate([sub(v, p), w2[p]], axis=0)))) for p in probs}
    m_mat = {p: jnp.where(eye > 0, p_end[p[0]][p[1]][:, psl(p[2])], 0.0)
             + pick(_bdot(zt[p], bf(jnp.concatenate([jnp.zeros_like(w1[p]), w1[p]], axis=0)))) for p in probs}
    rm = {p: bf(jnp.concatenate([r2[p], m_mat[p]], axis=0)) for p in probs}

    def seg_mean(x):
        s0 = jnp.sum(jnp.where(first, x, 0.0), axis=-1, keepdims=True)
        s1 = jnp.sum(jnp.where(first, 0.0, x), axis=-1, keepdims=True)
        return jnp.where(first, s0, s1) * (1.0 / c)

    hs = {(s, q): h_ref[s, q] for s in range(nseq) for q in range(npair)}
    for ci in range(nc):
        for s in range(nseq):
            for q in range(npair):
                p = (s, ci, q)
                yh = _bdot(rm[p], blockdiag(hs[(s, q)]))
                hs[(s, q)] = yh[c:] + g_mat[p]
                yy = yh[:c] + y2[p]
                yc = yy - seg_mean(yy)
                var = seg_mean(yc * yc)
                yn = yc * lax.rsqrt(var + RW_GN_EPS) * g_ref[:, psl(q)] + bb_ref[:, psl(q)]
                bonus = seg_mean(sub(bonus_w, p)) * float(c) * sub(v, p)
                y_ref[s, rsl(ci), psl(q)] = (yn + bonus).astype(y_ref.dtype)
    for s in range(nseq):
        for q in range(npair):
            h_ref[s, q] = hs[(s, q)]


def _out_kernel(ym_ref, yr_ref, gate_ref, x_ref, wo_ref, g_ref, o_ref):
    y = jnp.concatenate([ym_ref[...], yr_ref[...]], axis=-1) * gate_ref[...]
    out = jnp.dot(y, wo_ref[...], preferred_element_type=F32)
    o_ref[...] = x_ref[...] + _rms(out, g_ref[...])


def _full(shape):
    nd = len(shape)
    return pl.BlockSpec(shape, lambda *_: (0,) * nd)


def _rot_cols(w):
    half = w.shape[-1] // 2
    return jnp.concatenate([-w[..., half:], w[..., :half]], axis=-1)


def kernel(x, positions, norm_pre_g, w_in, mla_q_norm_g, mla_w_uq, mla_kv_norm_g, mla_w_ukv,
           rw_mu, rw_w0, rw_w2, rw_a0, rw_a2, rw_k_k, rw_k_a, rw_r_k, rw_ln_g, rw_ln_b,
           w_out, norm_post_g):
    bsz, seq, _ = x.shape
    n_tok = bsz * seq
    assert norm_pre_g.shape[0] == 1
    assert seq % TM_PROJ == 0 and seq % TQ == 0 and seq % TM_WKV == 0 and n_tok % TM_OUT == 0
    assert bsz % SEQ_WKV == 0
    row = lambda p: p.reshape(1, -1).astype(F32)

    x2 = x.reshape(n_tok, D_MODEL)
    pos = positions.reshape(n_tok // TM_PROJ, 1, TM_PROJ).astype(F32)
    inv_freq = ROPE_THETA ** (-jnp.arange(0, MLA_ROPE, 2, dtype=F32) / MLA_ROPE)
    invf = inv_freq.reshape(MLA_ROPE // 2, 1)

    w = w_in[0]
    w_kr = w[:, Q_LORA + KV_LORA:MLA_COLS]
    zeros64 = jnp.zeros((D_MODEL, 64), F32)
    wa = jnp.concatenate([w[:, :Q_LORA + KV_LORA], w_kr, zeros64, _rot_cols(w_kr), zeros64], axis=1).astype(BF16)
    wrw = w[:, MLA_COLS:MLA_COLS + RW_SHIFT_COLS].astype(BF16)
    wz = w[:, MLA_COLS + RW_SHIFT_COLS:].astype(BF16)
    wq = mla_w_uq[0].reshape(Q_LORA, MLA_HEADS, MLA_NOPE + MLA_ROPE)
    wqn = wq[:, :, :MLA_NOPE].reshape(Q_LORA, MLA_HEADS * 128).astype(BF16)
    wq_rope = wq[:, :, MLA_NOPE:]
    pad64 = jnp.zeros((Q_LORA, MLA_HEADS, 64), F32)
    wqr = jnp.concatenate([wq_rope, pad64], axis=-1).reshape(Q_LORA, MLA_HEADS * 128).astype(BF16)
    wqt = jnp.concatenate([_rot_cols(wq_rope), pad64], axis=-1).reshape(Q_LORA, MLA_HEADS * 128).astype(BF16)
    wkv = mla_w_ukv[0].reshape(KV_LORA, MLA_HEADS, MLA_NOPE + MLA_V)
    wkn = wkv[:, :, :MLA_NOPE].reshape(KV_LORA, MLA_HEADS * 128).astype(BF16)
    wkvv = wkv[:, :, MLA_NOPE:].reshape(KV_LORA, MLA_HEADS * 128).astype(BF16)
    head_id = np.arange(RW_WIDTH) // RW_HEAD
    ones_bd = jnp.asarray(head_id[:, None] == head_id[None, :], dtype=BF16)

    tm = TM_PROJ
    tok = lambda cols: pl.BlockSpec((tm, cols), lambda i: (i, 0))
    head_major = lambda cols: pl.BlockSpec((MLA_HEADS, tm, cols), lambda i: (0, i, 0))
    proj_in = [x2, pos, row(norm_pre_g), wa, wrw, wz,
               row(mla_q_norm_g), wqn, wqr, wqt, row(mla_kv_norm_g), wkn, wkvv, invf,
               row(rw_mu), row(rw_w0), rw_w2[0].astype(BF16), row(rw_a0), rw_a2[0].astype(BF16),
               row(rw_k_k), row(rw_k_a), ones_bd]
    proj_in_specs = ([tok(D_MODEL), pl.BlockSpec((1, 1, tm), lambda i: (i, 0, 0))]
                     + [_full(a.shape) for a in proj_in[2:]])
    rw_shape = jax.ShapeDtypeStruct((n_tok, RW_WIDTH), F32)
    q, k, v, r, lw, kp, vr, kk, b, z = pl.pallas_call(
        functools.partial(_proj_kernel, seq // tm),
        grid=(n_tok // tm,),
        in_specs=proj_in_specs,
        out_specs=[head_major(QK_PAD), head_major(QK_PAD), head_major(MLA_V)]
                  + [tok(RW_WIDTH)] * 6 + [tok(D_MODEL)],
        out_shape=[jax.ShapeDtypeStruct((MLA_HEADS, n_tok, QK_PAD), BF16),
                   jax.ShapeDtypeStruct((MLA_HEADS, n_tok, QK_PAD), BF16),
                   jax.ShapeDtypeStruct((MLA_HEADS, n_tok, MLA_V), BF16)]
                  + [rw_shape] * 6 + [jax.ShapeDtypeStruct((n_tok, D_MODEL), BF16)],
        scratch_shapes=[pltpu.VMEM((tm + 8, RW_SHIFT_COLS), F32)],
        compiler_params=pltpu.CompilerParams(dimension_semantics=("arbitrary",),
                                             vmem_limit_bytes=VMEM_LIMIT),
        name="proj",
    )(*proj_in)

    y_mla = pl.pallas_call(
        _attn_kernel,
        grid=(bsz, MLA_HEADS),
        in_specs=[pl.BlockSpec((1, seq, QK_PAD), lambda bi, h: (h, bi, 0)),
                  pl.BlockSpec((1, seq, QK_PAD), lambda bi, h: (h, bi, 0)),
                  pl.BlockSpec((1, seq, MLA_V), lambda bi, h: (h, bi, 0))],
        out_specs=pl.BlockSpec((seq, MLA_V), lambda bi, h: (bi, h)),
        out_shape=jax.ShapeDtypeStruct((n_tok, MLA_WIDTH), BF16),
        compiler_params=pltpu.CompilerParams(dimension_semantics=("arbitrary",) * 2,
                                             vmem_limit_bytes=VMEM_LIMIT),
        name="attn",
    )(q, k, v)

    nt = seq // TM_WKV
    tok_id = np.arange(TM_WKV)
    tri = jnp.asarray((tok_id[:, None] >= tok_id[None, :]) & (tok_id[:, None] // CHUNK == tok_id[None, :] // CHUNK),
                      dtype=BF16)
    tile_spec = pl.BlockSpec((SEQ_WKV, TM_WKV, RW_WIDTH), lambda bi, ti: (bi, ti, 0))
    per_seq = lambda a: a.reshape(bsz, seq, RW_WIDTH)
    wkv_in = [per_seq(a) for a in (r, lw, kp, vr, kk, b)] + [tri, row(rw_r_k), row(rw_ln_g), row(rw_ln_b)]
    y_rw = pl.pallas_call(
        _wkv_kernel,
        grid=(bsz // SEQ_WKV, nt),
        in_specs=[tile_spec] * 6 + [_full(a.shape) for a in wkv_in[6:]],
        out_specs=tile_spec,
        out_shape=jax.ShapeDtypeStruct((bsz, seq, RW_WIDTH), BF16),
        scratch_shapes=[pltpu.VMEM((SEQ_WKV, RW_WIDTH // 128, RW_HEAD, 128), F32)],
        compiler_params=pltpu.CompilerParams(dimension_semantics=("arbitrary", "arbitrary"),
                                             vmem_limit_bytes=VMEM_LIMIT),
        name="wkv",
    )(*wkv_in)

    tmo = TM_OUT
    toko = lambda cols: pl.BlockSpec((tmo, cols), lambda i: (i, 0))
    out = pl.pallas_call(
        _out_kernel,
        grid=(n_tok // tmo,),
        in_specs=[toko(MLA_WIDTH), toko(RW_WIDTH), toko(D_MODEL), toko(D_MODEL),
                  _full((D_MODEL, D_MODEL)), _full((1, D_MODEL))],
        out_specs=toko(D_MODEL),
        out_shape=jax.ShapeDtypeStruct((n_tok, D_MODEL), F32),
        compiler_params=pltpu.CompilerParams(dimension_semantics=("arbitrary",),
                                             vmem_limit_bytes=VMEM_LIMIT),
        name="outproj",
    )(y_mla, y_rw.reshape(n_tok, RW_WIDTH), z, x2, w_out[0].astype(BF16), row(norm_post_g))
    return out.reshape(bsz, seq, D_MODEL)
```

```python
import functools

import jax
import jax.numpy as jnp
import numpy as np
from jax import lax
from jax.experimental import pallas as pl
from jax.experimental.pallas import tpu as pltpu

D_MODEL = 1024
MLA_HEADS = 4
MLA_NOPE = 128
MLA_ROPE = 64
MLA_V = 128
MLA_WIDTH = MLA_HEADS * MLA_V
Q_LORA = 256
KV_LORA = 128
ROPE_THETA = 10000.0
RW_HEAD = 64
RW_WIDTH = 512
RW_HEADS = 8
W_LORA = 64
A_LORA = 64
RW_GN_EPS = 64e-5
NORM_EPS = 1e-6
MLA_COLS = Q_LORA + KV_LORA + MLA_ROPE
RW_SHIFT_COLS = 3 * RW_WIDTH + W_LORA + A_LORA
QK_PAD = 256
WA_COLS = Q_LORA + KV_LORA + 128 + 128

CHUNK = 64
TM_WKV = 256
SEQ_WKV = 2
TM_PROJ = 512
PROJ_SPLIT = 2
TQ = 256
TM_OUT = 512
VMEM_LIMIT = 48 * 1024 * 1024

F32 = jnp.float32
BF16 = jnp.bfloat16


def _rms(x, g):
    return x * lax.rsqrt(jnp.mean(x * x, axis=-1, keepdims=True) + NORM_EPS) * g


def _proj_kernel(tiles_per_seq, x_ref, pos_ref, *refs):
    params, outs, shift_ref = refs[:20], refs[20:30], refs[30]
    tm = x_ref.shape[0]
    th = tm // PROJ_SPLIT
    for g in range(PROJ_SPLIT):
        rows = pl.ds(g * th, th)
        head_major = [o.at[:, rows] for o in outs[:3]]
        token_major = [o.at[rows] for o in outs[3:]]
        _proj_rows(tiles_per_seq, g * th, tm, x_ref.at[rows], pos_ref.at[:, :, rows], *params,
                   *head_major, *token_major, shift_ref)


def _proj_rows(tiles_per_seq, row0, tile_rows,
               x_ref, pos_ref, gpre_ref, wa_ref, wrw_ref, wz_ref,
               qg_ref, wqn_ref, wqr_ref, wqt_ref, kvg_ref, wkn_ref, wkv_ref, invf_ref,
               mu_ref, w0_ref, w2_ref, a0_ref, a2_ref, kkw_ref, ka_ref, ones_ref,
               q_ref, k_ref, v_ref, r_ref, lw_ref, kp_ref, vr_ref, kk_ref, b_ref, z_ref,
               shift_ref):
    tm = x_ref.shape[0]
    i = pl.program_id(0)
    scale = float((MLA_NOPE + MLA_ROPE) ** -0.5 * np.log2(np.e))

    u = _rms(x_ref[...], gpre_ref[...]).astype(BF16)

    pa = jnp.dot(u, wa_ref[...], preferred_element_type=F32)
    c_q = pa[:, :Q_LORA]
    c_kv = pa[:, Q_LORA:Q_LORA + KV_LORA]
    kr = pa[:, Q_LORA + KV_LORA:Q_LORA + KV_LORA + 128]
    kr_rot = pa[:, Q_LORA + KV_LORA + 128:]

    ang = invf_ref[...] * pos_ref[0]
    cos_t = jnp.cos(ang)
    sin_t = jnp.sin(ang)
    cos = jnp.concatenate([cos_t] * 4, axis=0).T
    sin = jnp.concatenate([sin_t] * 4, axis=0).T

    cqn = _rms(c_q, qg_ref[...]).astype(BF16)
    qn = jnp.dot(cqn, wqn_ref[...], preferred_element_type=F32)
    qr = jnp.dot(cqn, wqr_ref[...], preferred_element_type=F32)
    qt = jnp.dot(cqn, wqt_ref[...], preferred_element_type=F32)
    ckn = _rms(c_kv, kvg_ref[...]).astype(BF16)
    kn = jnp.dot(ckn, wkn_ref[...], preferred_element_type=F32)
    vv = jnp.dot(ckn, wkv_ref[...], preferred_element_type=F32)
    k_rope = (kr * cos + kr_rot * sin).astype(BF16)
    for h in range(MLA_HEADS):
        sl = slice(128 * h, 128 * (h + 1))
        q_ref[h, :, :128] = (qn[:, sl] * scale).astype(BF16)
        q_ref[h, :, 128:] = ((qr[:, sl] * cos + qt[:, sl] * sin) * scale).astype(BF16)
        k_ref[h, :, :128] = kn[:, sl].astype(BF16)
        k_ref[h, :, 128:] = k_rope
        v_ref[h] = vv[:, sl].astype(BF16)

    z = jnp.dot(u, wz_ref[...], preferred_element_type=F32)
    z_ref[...] = (z / (1.0 + jnp.exp(-z))).astype(BF16)

    prw = jnp.dot(u, wrw_ref[...], preferred_element_type=F32)

    if row0 == 0:
        @pl.when(i % tiles_per_seq == 0)
        def _():
            shift_ref[0:8, :] = jnp.zeros((8, RW_SHIFT_COLS), F32)

        @pl.when(i % tiles_per_seq != 0)
        def _():
            shift_ref[7:8, :] = shift_ref[tile_rows + 7:tile_rows + 8, :]

    shift_ref[row0 + 8:row0 + tm + 8, :] = prw
    prev = shift_ref[row0 + 7:row0 + tm + 7, :]
    ps = prw + (prev - prw) * mu_ref[...]
    r = ps[:, :RW_WIDTH]
    k = ps[:, RW_WIDTH:2 * RW_WIDTH]
    v = ps[:, 2 * RW_WIDTH:3 * RW_WIDTH]
    xw = ps[:, 3 * RW_WIDTH:3 * RW_WIDTH + W_LORA]
    xa = ps[:, 3 * RW_WIDTH + W_LORA:]

    t = -(w0_ref[...] + jnp.dot(jnp.tanh(xw).astype(BF16), w2_ref[...], preferred_element_type=F32))
    softplus = jnp.maximum(t, 0.0) + jnp.log1p(jnp.exp(-jnp.abs(t)))
    w_log = -softplus - 0.5
    lw_ref[...] = -jnp.exp(w_log)
    a_pre = a0_ref[...] + jnp.dot(xa.astype(BF16), a2_ref[...], preferred_element_type=F32)
    a = 1.0 / (1.0 + jnp.exp(-a_pre))
    kk = k * kkw_ref[...]
    sq = kk * kk
    ss = jnp.dot(sq.astype(BF16), ones_ref[...], preferred_element_type=F32)
    kk = kk / jnp.maximum(jnp.sqrt(ss), 1e-12)
    r_ref[...] = r
    kp_ref[...] = k * (1.0 + (a - 1.0) * ka_ref[...])
    vr_ref[...] = v
    kk_ref[...] = kk
    b_ref[...] = kk * a


def _attn_kernel(q_ref, k_ref, v_ref, o_ref):
    tq = TQ
    seq = q_ref.shape[1]
    row = lax.broadcasted_iota(jnp.int32, (tq, tq), 0)
    col = lax.broadcasted_iota(jnp.int32, (tq, tq), 1)
    causal = row >= col
    def scores(i):
        q = q_ref[0, i * tq:(i + 1) * tq, :]
        blocks = [lax.dot_general(q, k_ref[0, j * tq:(j + 1) * tq, :], (((1,), (1,)), ((), ())),
                                  preferred_element_type=F32) for j in range(i + 1)]
        blocks[i] = jnp.where(causal, blocks[i], -jnp.inf)
        return blocks

    nq = seq // tq
    ahead = scores(0)
    for i in range(nq):
        e = (i + 1) * tq
        blocks = ahead
        if i + 1 < nq:
            ahead = scores(i + 1)
        m = blocks[0]
        for sb in blocks[1:]:
            m = jnp.maximum(m, sb)
        m = jnp.max(m, axis=-1, keepdims=True)
        p = [jnp.exp2(sb - m) for sb in blocks]
        l = p[0]
        for pb in p[1:]:
            l = l + pb
        l = jnp.sum(l, axis=-1, keepdims=True)
        pcat = jnp.concatenate([pb.astype(BF16) for pb in p], axis=-1)
        o = jnp.dot(pcat, v_ref[0, :e, :], preferred_element_type=F32)
        o_ref[i * tq:e, :] = (o / l).astype(o_ref.dtype)


def _bdot(a, b):
    return jnp.dot(a, b, preferred_element_type=F32)


def _bdot_nt(a, b):
    return lax.dot_general(a, b, (((1,), (1,)), ((), ())), preferred_element_type=F32)


def _wkv_kernel(r_ref, lw_ref, k_ref, v_ref, kk_ref, b_ref, tri_ref, rk_ref, g_ref, bb_ref,
                y_ref, h_ref):
    c = CHUNK
    nseq, tm = r_ref.shape[0], r_ref.shape[1]
    nc = tm // c
    npair = RW_WIDTH // 128
    probs = [(s, ci, q) for s in range(nseq) for ci in range(nc) for q in range(npair)]
    bf = lambda x: x.astype(BF16)

    @pl.when(pl.program_id(1) == 0)
    def _():
        h_ref[...] = jnp.zeros(h_ref.shape, F32)

    tri = tri_ref[...]
    v, r_t, kap_t, k_h, b_h, bonus_w, zbar, p_end = [], [], [], [], [], [], [], []
    for s in range(nseq):
        lw = lw_ref[s]
        lw1 = lw.astype(BF16)
        rem = lw - lw1.astype(F32)
        lw2 = rem.astype(BF16)
        lw3 = (rem - lw2.astype(F32)).astype(BF16)
        cum = _bdot(tri, lw1) + _bdot(tri, lw2) + _bdot(tri, lw3)
        r = r_ref[s]
        k = k_ref[s]
        b = b_ref[s]
        e_neg = jnp.exp(-cum)
        v.append(v_ref[s])
        r_t.append(r * jnp.exp(cum))
        kap_t.append(kk_ref[s] * jnp.exp(cum - lw))
        k_h.append(k * e_neg)
        b_h.append(b * e_neg)
        bonus_w.append(r * k * rk_ref[...])
        zb, pe = [], []
        for ci in range(nc):
            rows = slice(ci * c, (ci + 1) * c)
            cum_end = cum[(ci + 1) * c - 1:(ci + 1) * c, :]
            e_bar = jnp.exp(cum_end - cum[rows])
            zb.append(jnp.concatenate([k[rows] * e_bar, -(b[rows] * e_bar)], axis=0))
            pe.append(jnp.exp(cum_end))
        zbar.append(zb)
        p_end.append(pe)

    i1 = lax.broadcasted_iota(jnp.int32, (c, 128), 0)
    j1 = lax.broadcasted_iota(jnp.int32, (c, 128), 1) % c
    i2 = lax.broadcasted_iota(jnp.int32, (2 * c, 128), 0)
    l2 = lax.broadcasted_iota(jnp.int32, (2 * c, 128), 1)
    j2 = l2 % c
    bd = (i2 // c) == (l2 // c)
    low2 = ((i2 < c) & (i2 > j2)) | (i2 - c >= j2)
    first = j1 == lax.broadcasted_iota(jnp.int32, (c, 128), 1)
    eye = (i1 == j1).astype(F32)
    d8 = (i1 // 8) == (j1 // 8)
    offs = []
    blk = 8
    while blk < c:
        offs.append(((i1 // (2 * blk)) == (j1 // (2 * blk))) & ((i1 // blk) != (j1 // blk)))
        blk *= 2

    def psl(q):
        return slice(128 * q, 128 * (q + 1))

    def rsl(ci):
        return slice(ci * c, (ci + 1) * c)

    def blockdiag(x):
        return bf(jnp.where(bd, jnp.concatenate([x, x], axis=0), 0.0))

    def pick(x):
        return jnp.where(first, x[:c], x[c:])

    sub = lambda a, p: a[p[0]][rsl(p[1]), psl(p[2])]

    x_cat = {p: bf(jnp.concatenate([sub(kap_t, p), sub(r_t, p)], axis=0)) for p in probs}
    a_b = {p: jnp.where(low2, _bdot_nt(x_cat[p], blockdiag(sub(b_h, p))), 0.0) for p in probs}
    a_k = {p: bf(jnp.where(low2, _bdot_nt(x_cat[p], blockdiag(sub(k_h, p))), 0.0)) for p in probs}
    a_kb = {p: a_b[p][:c] for p in probs}
    a_rb = {p: bf(a_b[p][c:]) for p in probs}

    d = {p: jnp.where(d8, a_kb[p], 0.0) for p in probs}
    d2 = {p: _bdot(bf(d[p]), blockdiag(d[p])) for p in probs}
    d4 = {p: _bdot(bf(d2[p]), blockdiag(d2[p])) for p in probs}
    t = {p: _bdot(bf(eye - d[p]), blockdiag(eye + d2[p])) for p in probs}
    t = {p: _bdot(bf(t[p]), blockdiag(eye + d4[p])) for p in probs}
    for off in offs:
        ta = {p: _bdot(bf(t[p]), blockdiag(jnp.where(off, a_kb[p], 0.0))) for p in probs}
        t = {p: t[p] - _bdot(bf(ta[p]), blockdiag(t[p])) for p in probs}
    tb = {p: bf(t[p]) for p in probs}

    vbd = {p: blockdiag(sub(v, p)) for p in probs}
    akv = {p: _bdot(a_k[p], vbd[p]) for p in probs}
    w1 = {p: _bdot(tb[p], blockdiag(sub(kap_t, p))) for p in probs}
    w2 = {p: _bdot(tb[p], blockdiag(akv[p][:c])) for p in probs}
    r2 = {p: sub(r_t, p) - _bdot(a_rb[p], blockdiag(w1[p])) for p in probs}
    y2 = {p: akv[p][c:] - _bdot(a_rb[p], blockdiag(w2[p])) for p in probs}
    zt = {p: bf(zbar[p[0]][p[1]][:, psl(p[2])].T) for p in probs}
    g_mat = {p: pick(_bdot(zt[p], bf(jnp.concatenate([sub(v, p), w2[p]], axis=0)))) for p in probs}
    m_mat = {p: jnp.where(eye > 0, p_end[p[0]][p[1]][:, psl(p[2])], 0.0)
             + pick(_bdot(zt[p], bf(jnp.concatenate([jnp.zeros_like(w1[p]), w1[p]], axis=0)))) for p in probs}
    rm = {p: bf(jnp.concatenate([r2[p], m_mat[p]], axis=0)) for p in probs}

    def seg_mean(x):
        s0 = jnp.sum(jnp.where(first, x, 0.0), axis=-1, keepdims=True)
        s1 = jnp.sum(jnp.where(first, 0.0, x), axis=-1, keepdims=True)
        return jnp.where(first, s0, s1) * (1.0 / c)

    hs = {(s, q): h_ref[s, q] for s in range(nseq) for q in range(npair)}
    for ci in range(nc):
        for s in range(nseq):
            for q in range(npair):
                p = (s, ci, q)
                yh = _bdot(rm[p], blockdiag(hs[(s, q)]))
                hs[(s, q)] = yh[c:] + g_mat[p]
                yy = yh[:c] + y2[p]
                yc = yy - seg_mean(yy)
                var = seg_mean(yc * yc)
                yn = yc * lax.rsqrt(var + RW_GN_EPS) * g_ref[:, psl(q)] + bb_ref[:, psl(q)]
                bonus = seg_mean(sub(bonus_w, p)) * float(c) * sub(v, p)
                y_ref[s, rsl(ci), psl(q)] = (yn + bonus).astype(y_ref.dtype)
    for s in range(nseq):
        for q in range(npair):
            h_ref[s, q] = hs[(s, q)]


def _out_kernel(ym_ref, yr_ref, gate_ref, x_ref, wo_ref, g_ref, o_ref):
    y = jnp.concatenate([ym_ref[...], yr_ref[...]], axis=-1) * gate_ref[...]
    out = jnp.dot(y, wo_ref[...], preferred_element_type=F32)
    o_ref[...] = x_ref[...] + _rms(out, g_ref[...])


def _full(shape):
    nd = len(shape)
    return pl.BlockSpec(shape, lambda *_: (0,) * nd)


def _rot_cols(w):
    half = w.shape[-1] // 2
    return jnp.concatenate([-w[..., half:], w[..., :half]], axis=-1)


def kernel(x, positions, norm_pre_g, w_in, mla_q_norm_g, mla_w_uq, mla_kv_norm_g, mla_w_ukv,
           rw_mu, rw_w0, rw_w2, rw_a0, rw_a2, rw_k_k, rw_k_a, rw_r_k, rw_ln_g, rw_ln_b,
           w_out, norm_post_g):
    bsz, seq, _ = x.shape
    n_tok = bsz * seq
    assert norm_pre_g.shape[0] == 1
    assert seq % TM_PROJ == 0 and seq % TQ == 0 and seq % TM_WKV == 0 and n_tok % TM_OUT == 0
    assert bsz % SEQ_WKV == 0
    row = lambda p: p.reshape(1, -1).astype(F32)

    x2 = x.reshape(n_tok, D_MODEL)
    pos = positions.reshape(n_tok // TM_PROJ, 1, TM_PROJ).astype(F32)
    inv_freq = ROPE_THETA ** (-jnp.arange(0, MLA_ROPE, 2, dtype=F32) / MLA_ROPE)
    invf = inv_freq.reshape(MLA_ROPE // 2, 1)

    w = w_in[0]
    w_kr = w[:, Q_LORA + KV_LORA:MLA_COLS]
    zeros64 = jnp.zeros((D_MODEL, 64), F32)
    wa = jnp.concatenate([w[:, :Q_LORA + KV_LORA], w_kr, zeros64, _rot_cols(w_kr), zeros64], axis=1).astype(BF16)
    wrw = w[:, MLA_COLS:MLA_COLS + RW_SHIFT_COLS].astype(BF16)
    wz = w[:, MLA_COLS + RW_SHIFT_COLS:].astype(BF16)
    wq = mla_w_uq[0].reshape(Q_LORA, MLA_HEADS, MLA_NOPE + MLA_ROPE)
    wqn = wq[:, :, :MLA_NOPE].reshape(Q_LORA, MLA_HEADS * 128).astype(BF16)
    wq_rope = wq[:, :, MLA_NOPE:]
    pad64 = jnp.zeros((Q_LORA, MLA_HEADS, 64), F32)
    wqr = jnp.concatenate([wq_rope, pad64], axis=-1).reshape(Q_LORA, MLA_HEADS * 128).astype(BF16)
    wqt = jnp.concatenate([_rot_cols(wq_rope), pad64], axis=-1).reshape(Q_LORA, MLA_HEADS * 128).astype(BF16)
    wkv = mla_w_ukv[0].reshape(KV_LORA, MLA_HEADS, MLA_NOPE + MLA_V)
    wkn = wkv[:, :, :MLA_NOPE].reshape(KV_LORA, MLA_HEADS * 128).astype(BF16)
    wkvv = wkv[:, :, MLA_NOPE:].reshape(KV_LORA, MLA_HEADS * 128).astype(BF16)
    head_id = np.arange(RW_WIDTH) // RW_HEAD
    ones_bd = jnp.asarray(head_id[:, None] == head_id[None, :], dtype=BF16)

    tm = TM_PROJ
    tok = lambda cols: pl.BlockSpec((tm, cols), lambda i: (i, 0))
    head_major = lambda cols: pl.BlockSpec((MLA_HEADS, tm, cols), lambda i: (0, i, 0))
    proj_in = [x2, pos, row(norm_pre_g), wa, wrw, wz,
               row(mla_q_norm_g), wqn, wqr, wqt, row(mla_kv_norm_g), wkn, wkvv, invf,
               row(rw_mu), row(rw_w0), rw_w2[0].astype(BF16), row(rw_a0), rw_a2[0].astype(BF16),
               row(rw_k_k), row(rw_k_a), ones_bd]
    proj_in_specs = ([tok(D_MODEL), pl.BlockSpec((1, 1, tm), lambda i: (i, 0, 0))]
                     + [_full(a.shape) for a in proj_in[2:]])
    rw_shape = jax.ShapeDtypeStruct((n_tok, RW_WIDTH), F32)
    q, k, v, r, lw, kp, vr, kk, b, z = pl.pallas_call(
        functools.partial(_proj_kernel, seq // tm),
        grid=(n_tok // tm,),
        in_specs=proj_in_specs,
        out_specs=[head_major(QK_PAD), head_major(QK_PAD), head_major(MLA_V)]
                  + [tok(RW_WIDTH)] * 6 + [tok(D_MODEL)],
        out_shape=[jax.ShapeDtypeStruct((MLA_HEADS, n_tok, QK_PAD), BF16),
                   jax.ShapeDtypeStruct((MLA_HEADS, n_tok, QK_PAD), BF16),
                   jax.ShapeDtypeStruct((MLA_HEADS, n_tok, MLA_V), BF16)]
                  + [rw_shape] * 6 + [jax.ShapeDtypeStruct((n_tok, D_MODEL), BF16)],
        scratch_shapes=[pltpu.VMEM((tm + 8, RW_SHIFT_COLS), F32)],
        compiler_params=pltpu.CompilerParams(dimension_semantics=("arbitrary",),
                                             vmem_limit_bytes=VMEM_LIMIT),
        name="proj",
    )(*proj_in)

    y_mla = pl.pallas_call(
        _attn_kernel,
        grid=(bsz, MLA_HEADS),
        in_specs=[pl.BlockSpec((1, seq, QK_PAD), lambda bi, h: (h, bi, 0)),
                  pl.BlockSpec((1, seq, QK_PAD), lambda bi, h: (h, bi, 0)),
                  pl.BlockSpec((1, seq, MLA_V), lambda bi, h: (h, bi, 0))],
        out_specs=pl.BlockSpec((seq, MLA_V), lambda bi, h: (bi, h)),
        out_shape=jax.ShapeDtypeStruct((n_tok, MLA_WIDTH), BF16),
        compiler_params=pltpu.CompilerParams(dimension_semantics=("arbitrary",) * 2,
                                             vmem_limit_bytes=VMEM_LIMIT),
        name="attn",
    )(q, k, v)

    nt = seq // TM_WKV
    tok_id = np.arange(TM_WKV)
    tri = jnp.asarray((tok_id[:, None] >= tok_id[None, :]) & (tok_id[:, None] // CHUNK == tok_id[None, :] // CHUNK),
                      dtype=BF16)
    tile_spec = pl.BlockSpec((SEQ_WKV, TM_WKV, RW_WIDTH), lambda bi, ti: (bi, ti, 0))
    per_seq = lambda a: a.reshape(bsz, seq, RW_WIDTH)
    wkv_in = [per_seq(a) for a in (r, lw, kp, vr, kk, b)] + [tri, row(rw_r_k), row(rw_ln_g), row(rw_ln_b)]
    y_rw = pl.pallas_call(
        _wkv_kernel,
        grid=(bsz // SEQ_WKV, nt),
        in_specs=[tile_spec] * 6 + [_full(a.shape) for a in wkv_in[6:]],
        out_specs=tile_spec,
        out_shape=jax.ShapeDtypeStruct((bsz, seq, RW_WIDTH), BF16),
        scratch_shapes=[pltpu.VMEM((SEQ_WKV, RW_WIDTH // 128, RW_HEAD, 128), F32)],
        compiler_params=pltpu.CompilerParams(dimension_semantics=("arbitrary", "arbitrary"),
                                             vmem_limit_bytes=VMEM_LIMIT),
        name="wkv",
    )(*wkv_in)

    tmo = TM_OUT
    toko = lambda cols: pl.BlockSpec((tmo, cols), lambda i: (i, 0))
    out = pl.pallas_call(
        _out_kernel,
        grid=(n_tok // tmo,),
        in_specs=[toko(MLA_WIDTH), toko(RW_WIDTH), toko(D_MODEL), toko(D_MODEL),
                  _full((D_MODEL, D_MODEL)), _full((1, D_MODEL))],
        out_specs=toko(D_MODEL),
        out_shape=jax.ShapeDtypeStruct((n_tok, D_MODEL), F32),
        compiler_params=pltpu.CompilerParams(dimension_semantics=("arbitrary",),
                                             vmem_limit_bytes=VMEM_LIMIT),
        name="outproj",
    )(y_mla, y_rw.reshape(n_tok, RW_WIDTH), z, x2, w_out[0].astype(BF16), row(norm_post_g))
    return out.reshape(bsz, seq, D_MODEL)
```

```python
import functools

import jax
import jax.numpy as jnp
import numpy as np
from jax import lax
from jax.experimental import pallas as pl
from jax.experimental.pallas import tpu as pltpu

D_MODEL = 1024
MLA_HEADS = 4
MLA_NOPE = 128
MLA_ROPE = 64
MLA_V = 128
MLA_WIDTH = MLA_HEADS * MLA_V
Q_LORA = 256
KV_LORA = 128
ROPE_THETA = 10000.0
RW_HEAD = 64
RW_WIDTH = 512
RW_HEADS = 8
W_LORA = 64
A_LORA = 64
RW_GN_EPS = 64e-5
NORM_EPS = 1e-6
MLA_COLS = Q_LORA + KV_LORA + MLA_ROPE
RW_SHIFT_COLS = 3 * RW_WIDTH + W_LORA + A_LORA
QK_PAD = 256
WA_COLS = Q_LORA + KV_LORA + 128 + 128

CHUNK = 64
TM_WKV = 256
SEQ_WKV = 2
TM_PROJ = 512
PROJ_SPLIT = 2
TQ = 256
TM_OUT = 512
VMEM_LIMIT = 48 * 1024 * 1024

F32 = jnp.float32
BF16 = jnp.bfloat16


def _rms(x, g):
    return x * lax.rsqrt(jnp.mean(x * x, axis=-1, keepdims=True) + NORM_EPS) * g


def _proj_kernel(tiles_per_seq, x_ref, pos_ref, *refs):
    params, outs, shift_ref = refs[:20], refs[20:30], refs[30]
    tm = x_ref.shape[0]
    th = tm // PROJ_SPLIT
    for g in range(PROJ_SPLIT):
        rows = pl.ds(g * th, th)
        head_major = [o.at[:, rows] for o in outs[:3]]
        token_major = [o.at[rows] for o in outs[3:]]
        _proj_rows(tiles_per_seq, g * th, tm, x_ref.at[rows], pos_ref.at[:, :, rows], *params,
                   *head_major, *token_major, shift_ref)


def _proj_rows(tiles_per_seq, row0, tile_rows,
               x_ref, pos_ref, gpre_ref, wa_ref, wrw_ref, wz_ref,
               qg_ref, wqn_ref, wqr_ref, wqt_ref, kvg_ref, wkn_ref, wkv_ref, invf_ref,
               mu_ref, w0_ref, w2_ref, a0_ref, a2_ref, kkw_ref, ka_ref, ones_ref,
               q_ref, k_ref, v_ref, r_ref, lw_ref, kp_ref, vr_ref, kk_ref, b_ref, z_ref,
               shift_ref):
    tm = x_ref.shape[0]
    i = pl.program_id(0)
    scale = float((MLA_NOPE + MLA_ROPE) ** -0.5 * np.log2(np.e))

    u = _rms(x_ref[...], gpre_ref[...]).astype(BF16)

    pa = jnp.dot(u, wa_ref[...], preferred_element_type=F32)
    c_q = pa[:, :Q_LORA]
    c_kv = pa[:, Q_LORA:Q_LORA + KV_LORA]
    kr = pa[:, Q_LORA + KV_LORA:Q_LORA + KV_LORA + 128]
    kr_rot = pa[:, Q_LORA + KV_LORA + 128:]

    ang = invf_ref[...] * pos_ref[0]
    cos_t = jnp.cos(ang)
    sin_t = jnp.sin(ang)
    cos = jnp.concatenate([cos_t] * 4, axis=0).T
    sin = jnp.concatenate([sin_t] * 4, axis=0).T

    cqn = _rms(c_q, qg_ref[...]).astype(BF16)
    qn = jnp.dot(cqn, wqn_ref[...], preferred_element_type=F32)
    qr = jnp.dot(cqn, wqr_ref[...], preferred_element_type=F32)
    qt = jnp.dot(cqn, wqt_ref[...], preferred_element_type=F32)
    ckn = _rms(c_kv, kvg_ref[...]).astype(BF16)
    kn = jnp.dot(ckn, wkn_ref[...], preferred_element_type=F32)
    vv = jnp.dot(ckn, wkv_ref[...], preferred_element_type=F32)
    k_rope = (kr * cos + kr_rot * sin).astype(BF16)
    for h in range(MLA_HEADS):
        sl = slice(128 * h, 128 * (h + 1))
        q_ref[h, :, :128] = (qn[:, sl] * scale).astype(BF16)
        q_ref[h, :, 128:] = ((qr[:, sl] * cos + qt[:, sl] * sin) * scale).astype(BF16)
        k_ref[h, :, :128] = kn[:, sl].astype(BF16)
        k_ref[h, :, 128:] = k_rope
        v_ref[h] = vv[:, sl].astype(BF16)

    z = jnp.dot(u, wz_ref[...], preferred_element_type=F32)
    z_ref[...] = (z / (1.0 + jnp.exp(-z))).astype(BF16)

    prw = jnp.dot(u, wrw_ref[...], preferred_element_type=F32)

    if row0 == 0:
        @pl.when(i % tiles_per_seq == 0)
        def _():
            shift_ref[0:8, :] = jnp.zeros((8, RW_SHIFT_COLS), F32)

        @pl.when(i % tiles_per_seq != 0)
        def _():
            shift_ref[7:8, :] = shift_ref[tile_rows + 7:tile_rows + 8, :]

    shift_ref[row0 + 8:row0 + tm + 8, :] = prw
    prev = shift_ref[row0 + 7:row0 + tm + 7, :]
    ps = prw + (prev - prw) * mu_ref[...]
    r = ps[:, :RW_WIDTH]
    k = ps[:, RW_WIDTH:2 * RW_WIDTH]
    v = ps[:, 2 * RW_WIDTH:3 * RW_WIDTH]
    xw = ps[:, 3 * RW_WIDTH:3 * RW_WIDTH + W_LORA]
    xa = ps[:, 3 * RW_WIDTH + W_LORA:]

    t = -(w0_ref[...] + jnp.dot(jnp.tanh(xw).astype(BF16), w2_ref[...], preferred_element_type=F32))
    softplus = jnp.maximum(t, 0.0) + jnp.log1p(jnp.exp(-jnp.abs(t)))
    w_log = -softplus - 0.5
    lw_ref[...] = -jnp.exp(w_log)
    a_pre = a0_ref[...] + jnp.dot(xa.astype(BF16), a2_ref[...], preferred_element_type=F32)
    a = 1.0 / (1.0 + jnp.exp(-a_pre))
    kk = k * kkw_ref[...]
    sq = kk * kk
    ss = jnp.dot(sq.astype(BF16), ones_ref[...], preferred_element_type=F32)
    kk = kk / jnp.maximum(jnp.sqrt(ss), 1e-12)
    r_ref[...] = r.astype(r_ref.dtype)
    kp_ref[...] = (k * (1.0 + (a - 1.0) * ka_ref[...])).astype(kp_ref.dtype)
    vr_ref[...] = v.astype(vr_ref.dtype)
    kk_ref[...] = kk.astype(kk_ref.dtype)
    b_ref[...] = (kk * a).astype(b_ref.dtype)


def _attn_kernel(q_ref, k_ref, v_ref, o_ref):
    tq = TQ
    seq = q_ref.shape[1]
    row = lax.broadcasted_iota(jnp.int32, (tq, tq), 0)
    col = lax.broadcasted_iota(jnp.int32, (tq, tq), 1)
    causal = row >= col
    def scores(i):
        q = q_ref[0, i * tq:(i + 1) * tq, :]
        blocks = [lax.dot_general(q, k_ref[0, j * tq:(j + 1) * tq, :], (((1,), (1,)), ((), ())),
                                  preferred_element_type=F32) for j in range(i + 1)]
        blocks[i] = jnp.where(causal, blocks[i], -jnp.inf)
        return blocks

    nq = seq // tq
    ahead = scores(0)
    for i in range(nq):
        e = (i + 1) * tq
        blocks = ahead
        if i + 1 < nq:
            ahead = scores(i + 1)
        m = blocks[0]
        for sb in blocks[1:]:
            m = jnp.maximum(m, sb)
        m = jnp.max(m, axis=-1, keepdims=True)
        p = [jnp.exp2(sb - m) for sb in blocks]
        l = p[0]
        for pb in p[1:]:
            l = l + pb
        l = jnp.sum(l, axis=-1, keepdims=True)
        pcat = jnp.concatenate([pb.astype(BF16) for pb in p], axis=-1)
        o = jnp.dot(pcat, v_ref[0, :e, :], preferred_element_type=F32)
        o_ref[i * tq:e, :] = (o / l).astype(o_ref.dtype)


def _bdot(a, b):
    return jnp.dot(a, b, preferred_element_type=F32)


def _bdot_nt(a, b):
    return lax.dot_general(a, b, (((1,), (1,)), ((), ())), preferred_element_type=F32)


def _wkv_kernel(r_ref, lw_ref, k_ref, v_ref, kk_ref, b_ref, tri_ref, rk_ref, g_ref, bb_ref,
                y_ref, h_ref):
    c = CHUNK
    nseq, tm = r_ref.shape[0], r_ref.shape[1]
    nc = tm // c
    npair = RW_WIDTH // 128
    probs = [(s, ci, q) for s in range(nseq) for ci in range(nc) for q in range(npair)]
    bf = lambda x: x.astype(BF16)

    @pl.when(pl.program_id(1) == 0)
    def _():
        h_ref[...] = jnp.zeros(h_ref.shape, F32)

    tri = tri_ref[...]
    v, r_t, kap_t, k_h, b_h, bonus_w, zbar, p_end = [], [], [], [], [], [], [], []
    for s in range(nseq):
        lw = lw_ref[s]
        lw1 = lw.astype(BF16)
        rem = lw - lw1.astype(F32)
        lw2 = rem.astype(BF16)
        lw3 = (rem - lw2.astype(F32)).astype(BF16)
        cum = _bdot(tri, lw1) + _bdot(tri, lw2) + _bdot(tri, lw3)
        r = r_ref[s].astype(F32)
        k = k_ref[s].astype(F32)
        b = b_ref[s].astype(F32)
        e_neg = jnp.exp(-cum)
        v.append(v_ref[s].astype(F32))
        r_t.append(r * jnp.exp(cum))
        kap_t.append(kk_ref[s].astype(F32) * jnp.exp(cum - lw))
        k_h.append(k * e_neg)
        b_h.append(b * e_neg)
        bonus_w.append(r * k * rk_ref[...])
        zb, pe = [], []
        for ci in range(nc):
            rows = slice(ci * c, (ci + 1) * c)
            cum_end = cum[(ci + 1) * c - 1:(ci + 1) * c, :]
            e_bar = jnp.exp(cum_end - cum[rows])
            zb.append(jnp.concatenate([k[rows] * e_bar, -(b[rows] * e_bar)], axis=0))
            pe.append(jnp.exp(cum_end))
        zbar.append(zb)
        p_end.append(pe)

    i1 = lax.broadcasted_iota(jnp.int32, (c, 128), 0)
    j1 = lax.broadcasted_iota(jnp.int32, (c, 128), 1) % c
    i2 = lax.broadcasted_iota(jnp.int32, (2 * c, 128), 0)
    l2 = lax.broadcasted_iota(jnp.int32, (2 * c, 128), 1)
    j2 = l2 % c
    bd = (i2 // c) == (l2 // c)
    low2 = ((i2 < c) & (i2 > j2)) | (i2 - c >= j2)
    first = j1 == lax.broadcasted_iota(jnp.int32, (c, 128), 1)
    eye = (i1 == j1).astype(F32)
    d8 = (i1 // 8) == (j1 // 8)
    offs = []
    blk = 8
    while blk < c:
        offs.append(((i1 // (2 * blk)) == (j1 // (2 * blk))) & ((i1 // blk) != (j1 // blk)))
        blk *= 2

    def psl(q):
        return slice(128 * q, 128 * (q + 1))

    def rsl(ci):
        return slice(ci * c, (ci + 1) * c)

    def blockdiag(x):
        return bf(jnp.where(bd, jnp.concatenate([x, x], axis=0), 0.0))

    def pick(x):
        return jnp.where(first, x[:c], x[c:])

    sub = lambda a, p: a[p[0]][rsl(p[1]), psl(p[2])]

    x_cat = {p: bf(jnp.concatenate([sub(kap_t, p), sub(r_t, p)], axis=0)) for p in probs}
    a_b = {p: jnp.where(low2, _bdot_nt(x_cat[p], blockdiag(sub(b_h, p))), 0.0) for p in probs}
    a_k = {p: bf(jnp.where(low2, _bdot_nt(x_cat[p], blockdiag(sub(k_h, p))), 0.0)) for p in probs}
    a_kb = {p: a_b[p][:c] for p in probs}
    a_rb = {p: bf(a_b[p][c:]) for p in probs}

    d = {p: jnp.where(d8, a_kb[p], 0.0) for p in probs}
    d2 = {p: _bdot(bf(d[p]), blockdiag(d[p])) for p in probs}
    d4 = {p: _bdot(bf(d2[p]), blockdiag(d2[p])) for p in probs}
    t = {p: _bdot(bf(eye - d[p]), blockdiag(eye + d2[p])) for p in probs}
    t = {p: _bdot(bf(t[p]), blockdiag(eye + d4[p])) for p in probs}
    for off in offs:
        ta = {p: _bdot(bf(t[p]), blockdiag(jnp.where(off, a_kb[p], 0.0))) for p in probs}
        t = {p: t[p] - _bdot(bf(ta[p]), blockdiag(t[p])) for p in probs}
    tb = {p: bf(t[p]) for p in probs}

    vbd = {p: blockdiag(sub(v, p)) for p in probs}
    akv = {p: _bdot(a_k[p], vbd[p]) for p in probs}
    w1 = {p: _bdot(tb[p], blockdiag(sub(kap_t, p))) for p in probs}
    w2 = {p: _bdot(tb[p], blockdiag(akv[p][:c])) for p in probs}
    r2 = {p: sub(r_t, p) - _bdot(a_rb[p], blockdiag(w1[p])) for p in probs}
    y2 = {p: akv[p][c:] - _bdot(a_rb[p], blockdiag(w2[p])) for p in probs}
    zt = {p: bf(zbar[p[0]][p[1]][:, psl(p[2])].T) for p in probs}
    g_mat = {p: pick(_bdot(zt[p], bf(jnp.concatenate([sub(v, p), w2[p]], axis=0)))) for p in probs}
    m_mat = {p: jnp.where(eye > 0, p_end[p[0]][p[1]][:, psl(p[2])], 0.0)
             + pick(_bdot(zt[p], bf(jnp.concatenate([jnp.zeros_like(w1[p]), w1[p]], axis=0)))) for p in probs}
    rm = {p: bf(jnp.concatenate([r2[p], m_mat[p]], axis=0)) for p in probs}

    def seg_mean(x):
        s0 = jnp.sum(jnp.where(first, x, 0.0), axis=-1, keepdims=True)
        s1 = jnp.sum(jnp.where(first, 0.0, x), axis=-1, keepdims=True)
        return jnp.where(first, s0, s1) * (1.0 / c)

    hs = {(s, q): h_ref[s, q] for s in range(nseq) for q in range(npair)}
    for ci in range(nc):
        for s in range(nseq):
            for q in range(npair):
                p = (s, ci, q)
                yh = _bdot(rm[p], blockdiag(hs[(s, q)]))
                hs[(s, q)] = yh[c:] + g_mat[p]
                yy = yh[:c] + y2[p]
                yc = yy - seg_mean(yy)
                var = seg_mean(yc * yc)
                yn = yc * lax.rsqrt(var + RW_GN_EPS) * g_ref[:, psl(q)] + bb_ref[:, psl(q)]
                bonus = seg_mean(sub(bonus_w, p)) * float(c) * sub(v, p)
                y_ref[s, rsl(ci), psl(q)] = (yn + bonus).astype(y_ref.dtype)
    for s in range(nseq):
        for q in range(npair):
            h_ref[s, q] = hs[(s, q)]


def _out_kernel(ym_ref, yr_ref, gate_ref, x_ref, wo_ref, g_ref, o_ref):
    y = jnp.concatenate([ym_ref[...], yr_ref[...]], axis=-1) * gate_ref[...]
    out = jnp.dot(y, wo_ref[...], preferred_element_type=F32)
    o_ref[...] = x_ref[...] + _rms(out, g_ref[...])


def _full(shape):
    nd = len(shape)
    return pl.BlockSpec(shape, lambda *_: (0,) * nd)


def _rot_cols(w):
    half = w.shape[-1] // 2
    return jnp.concatenate([-w[..., half:], w[..., :half]], axis=-1)


def kernel(x, positions, norm_pre_g, w_in, mla_q_norm_g, mla_w_uq, mla_kv_norm_g, mla_w_ukv,
           rw_mu, rw_w0, rw_w2, rw_a0, rw_a2, rw_k_k, rw_k_a, rw_r_k, rw_ln_g, rw_ln_b,
           w_out, norm_post_g):
    bsz, seq, _ = x.shape
    n_tok = bsz * seq
    assert norm_pre_g.shape[0] == 1
    assert seq % TM_PROJ == 0 and seq % TQ == 0 and seq % TM_WKV == 0 and n_tok % TM_OUT == 0
    assert bsz % SEQ_WKV == 0
    row = lambda p: p.reshape(1, -1).astype(F32)

    x2 = x.reshape(n_tok, D_MODEL)
    pos = positions.reshape(n_tok // TM_PROJ, 1, TM_PROJ).astype(F32)
    inv_freq = ROPE_THETA ** (-jnp.arange(0, MLA_ROPE, 2, dtype=F32) / MLA_ROPE)
    invf = inv_freq.reshape(MLA_ROPE // 2, 1)

    w = w_in[0]
    w_kr = w[:, Q_LORA + KV_LORA:MLA_COLS]
    zeros64 = jnp.zeros((D_MODEL, 64), F32)
    wa = jnp.concatenate([w[:, :Q_LORA + KV_LORA], w_kr, zeros64, _rot_cols(w_kr), zeros64], axis=1).astype(BF16)
    wrw = w[:, MLA_COLS:MLA_COLS + RW_SHIFT_COLS].astype(BF16)
    wz = w[:, MLA_COLS + RW_SHIFT_COLS:].astype(BF16)
    wq = mla_w_uq[0].reshape(Q_LORA, MLA_HEADS, MLA_NOPE + MLA_ROPE)
    wqn = wq[:, :, :MLA_NOPE].reshape(Q_LORA, MLA_HEADS * 128).astype(BF16)
    wq_rope = wq[:, :, MLA_NOPE:]
    pad64 = jnp.zeros((Q_LORA, MLA_HEADS, 64), F32)
    wqr = jnp.concatenate([wq_rope, pad64], axis=-1).reshape(Q_LORA, MLA_HEADS * 128).astype(BF16)
    wqt = jnp.concatenate([_rot_cols(wq_rope), pad64], axis=-1).reshape(Q_LORA, MLA_HEADS * 128).astype(BF16)
    wkv = mla_w_ukv[0].reshape(KV_LORA, MLA_HEADS, MLA_NOPE + MLA_V)
    wkn = wkv[:, :, :MLA_NOPE].reshape(KV_LORA, MLA_HEADS * 128).astype(BF16)
    wkvv = wkv[:, :, MLA_NOPE:].reshape(KV_LORA, MLA_HEADS * 128).astype(BF16)
    head_id = np.arange(RW_WIDTH) // RW_HEAD
    ones_bd = jnp.asarray(head_id[:, None] == head_id[None, :], dtype=BF16)

    tm = TM_PROJ
    tok = lambda cols: pl.BlockSpec((tm, cols), lambda i: (i, 0))
    head_major = lambda cols: pl.BlockSpec((MLA_HEADS, tm, cols), lambda i: (0, i, 0))
    proj_in = [x2, pos, row(norm_pre_g), wa, wrw, wz,
               row(mla_q_norm_g), wqn, wqr, wqt, row(mla_kv_norm_g), wkn, wkvv, invf,
               row(rw_mu), row(rw_w0), rw_w2[0].astype(BF16), row(rw_a0), rw_a2[0].astype(BF16),
               row(rw_k_k), row(rw_k_a), ones_bd]
    proj_in_specs = ([tok(D_MODEL), pl.BlockSpec((1, 1, tm), lambda i: (i, 0, 0))]
                     + [_full(a.shape) for a in proj_in[2:]])
    rw_shape = jax.ShapeDtypeStruct((n_tok, RW_WIDTH), F32)
    rw_bf16 = jax.ShapeDtypeStruct((n_tok, RW_WIDTH), BF16)
    q, k, v, r, lw, kp, vr, kk, b, z = pl.pallas_call(
        functools.partial(_proj_kernel, seq // tm),
        grid=(n_tok // tm,),
        in_specs=proj_in_specs,
        out_specs=[head_major(QK_PAD), head_major(QK_PAD), head_major(MLA_V)]
                  + [tok(RW_WIDTH)] * 6 + [tok(D_MODEL)],
        out_shape=[jax.ShapeDtypeStruct((MLA_HEADS, n_tok, QK_PAD), BF16),
                   jax.ShapeDtypeStruct((MLA_HEADS, n_tok, QK_PAD), BF16),
                   jax.ShapeDtypeStruct((MLA_HEADS, n_tok, MLA_V), BF16)]
                  + [rw_bf16, rw_shape, rw_bf16, rw_bf16, rw_bf16, rw_bf16]
                  + [jax.ShapeDtypeStruct((n_tok, D_MODEL), BF16)],
        scratch_shapes=[pltpu.VMEM((tm + 8, RW_SHIFT_COLS), F32)],
        compiler_params=pltpu.CompilerParams(dimension_semantics=("arbitrary",),
                                             vmem_limit_bytes=VMEM_LIMIT),
        name="proj",
    )(*proj_in)

    y_mla = pl.pallas_call(
        _attn_kernel,
        grid=(bsz, MLA_HEADS),
        in_specs=[pl.BlockSpec((1, seq, QK_PAD), lambda bi, h: (h, bi, 0)),
                  pl.BlockSpec((1, seq, QK_PAD), lambda bi, h: (h, bi, 0)),
                  pl.BlockSpec((1, seq, MLA_V), lambda bi, h: (h, bi, 0))],
        out_specs=pl.BlockSpec((seq, MLA_V), lambda bi, h: (bi, h)),
        out_shape=jax.ShapeDtypeStruct((n_tok, MLA_WIDTH), BF16),
        compiler_params=pltpu.CompilerParams(dimension_semantics=("arbitrary",) * 2,
                                             vmem_limit_bytes=VMEM_LIMIT),
        name="attn",
    )(q, k, v)

    nt = seq // TM_WKV
    tok_id = np.arange(TM_WKV)
    tri = jnp.asarray((tok_id[:, None] >= tok_id[None, :]) & (tok_id[:, None] // CHUNK == tok_id[None, :] // CHUNK),
                      dtype=BF16)
    tile_spec = pl.BlockSpec((SEQ_WKV, TM_WKV, RW_WIDTH), lambda bi, ti: (bi, ti, 0))
    per_seq = lambda a: a.reshape(bsz, seq, RW_WIDTH)
    wkv_in = [per_seq(a) for a in (r, lw, kp, vr, kk, b)] + [tri, row(rw_r_k), row(rw_ln_g), row(rw_ln_b)]
    y_rw = pl.pallas_call(
        _wkv_kernel,
        grid=(bsz // SEQ_WKV, nt),
        in_specs=[tile_spec] * 6 + [_full(a.shape) for a in wkv_in[6:]],
        out_specs=tile_spec,
        out_shape=jax.ShapeDtypeStruct((bsz, seq, RW_WIDTH), BF16),
        scratch_shapes=[pltpu.VMEM((SEQ_WKV, RW_WIDTH // 128, RW_HEAD, 128), F32)],
        compiler_params=pltpu.CompilerParams(dimension_semantics=("arbitrary", "arbitrary"),
                                             vmem_limit_bytes=VMEM_LIMIT),
        name="wkv",
    )(*wkv_in)

    tmo = TM_OUT
    toko = lambda cols: pl.BlockSpec((tmo, cols), lambda i: (i, 0))
    out = pl.pallas_call(
        _out_kernel,
        grid=(n_tok // tmo,),
        in_specs=[toko(MLA_WIDTH), toko(RW_WIDTH), toko(D_MODEL), toko(D_MODEL),
                  _full((D_MODEL, D_MODEL)), _full((1, D_MODEL))],
        out_specs=toko(D_MODEL),
        out_shape=jax.ShapeDtypeStruct((n_tok, D_MODEL), F32),
        compiler_params=pltpu.CompilerParams(dimension_semantics=("arbitrary",),
                                             vmem_limit_bytes=VMEM_LIMIT),
        name="outproj",
    )(y_mla, y_rw.reshape(n_tok, RW_WIDTH), z, x2, w_out[0].astype(BF16), row(norm_post_g))
    return out.reshape(bsz, seq, D_MODEL)
```

```python
import functools

import jax
import jax.numpy as jnp
import numpy as np
from jax import lax
from jax.experimental import pallas as pl
from jax.experimental.pallas import tpu as pltpu

D_MODEL = 1024
MLA_HEADS = 4
MLA_NOPE = 128
MLA_ROPE = 64
MLA_V = 128
MLA_WIDTH = MLA_HEADS * MLA_V
Q_LORA = 256
KV_LORA = 128
ROPE_THETA = 10000.0
RW_HEAD = 64
RW_WIDTH = 512
RW_HEADS = 8
W_LORA = 64
A_LORA = 64
RW_GN_EPS = 64e-5
NORM_EPS = 1e-6
MLA_COLS = Q_LORA + KV_LORA + MLA_ROPE
RW_SHIFT_COLS = 3 * RW_WIDTH + W_LORA + A_LORA
QK_PAD = 256
WA_COLS = Q_LORA + KV_LORA + 128 + 128

CHUNK = 64
TM_WKV = 256
SEQ_WKV = 2
TM_PROJ = 512
PROJ_SPLIT = 2
TQ = 256
ATTN_AHEAD = 4
TM_OUT = 512
VMEM_LIMIT = 48 * 1024 * 1024

F32 = jnp.float32
BF16 = jnp.bfloat16


def _rms(x, g):
    return x * lax.rsqrt(jnp.mean(x * x, axis=-1, keepdims=True) + NORM_EPS) * g


def _proj_kernel(tiles_per_seq, x_ref, pos_ref, *refs):
    params, outs, shift_ref = refs[:20], refs[20:30], refs[30]
    tm = x_ref.shape[0]
    th = tm // PROJ_SPLIT
    for g in range(PROJ_SPLIT):
        rows = pl.ds(g * th, th)
        head_major = [outs[0].at[:, rows], outs[1].at[:, rows], outs[2].at[:, :, rows]]
        token_major = [o.at[rows] for o in outs[3:]]
        _proj_rows(tiles_per_seq, g * th, tm, x_ref.at[rows], pos_ref.at[:, :, rows], *params,
                   *head_major, *token_major, shift_ref)


def _proj_rows(tiles_per_seq, row0, tile_rows,
               x_ref, pos_ref, gpre_ref, wa_ref, wrw_ref, wz_ref,
               qg_ref, wqn_ref, wqr_ref, wqt_ref, kvg_ref, wkn_ref, wkv_ref, invf_ref,
               mu_ref, w0_ref, w2_ref, a0_ref, a2_ref, kkw_ref, ka_ref, ones_ref,
               q_ref, k_ref, v_ref, r_ref, lw_ref, kp_ref, vr_ref, kk_ref, b_ref, z_ref,
               shift_ref):
    tm = x_ref.shape[0]
    i = pl.program_id(0)
    scale = float((MLA_NOPE + MLA_ROPE) ** -0.5 * np.log2(np.e))

    u = _rms(x_ref[...], gpre_ref[...]).astype(BF16)

    pa = jnp.dot(u, wa_ref[...], preferred_element_type=F32)
    c_q = pa[:, :Q_LORA]
    c_kv = pa[:, Q_LORA:Q_LORA + KV_LORA]
    kr = pa[:, Q_LORA + KV_LORA:Q_LORA + KV_LORA + 128]
    kr_rot = pa[:, Q_LORA + KV_LORA + 128:]

    ang = invf_ref[...] * pos_ref[0]
    cos_t = jnp.cos(ang)
    sin_t = jnp.sin(ang)
    cos = jnp.concatenate([cos_t] * 4, axis=0).T
    sin = jnp.concatenate([sin_t] * 4, axis=0).T

    cqn = _rms(c_q, qg_ref[...]).astype(BF16)
    qn = jnp.dot(cqn, wqn_ref[...], preferred_element_type=F32)
    qr = jnp.dot(cqn, wqr_ref[...], preferred_element_type=F32)
    qt = jnp.dot(cqn, wqt_ref[...], preferred_element_type=F32)
    ckn = _rms(c_kv, kvg_ref[...]).astype(BF16)
    kn = jnp.dot(ckn, wkn_ref[...], preferred_element_type=F32)
    vv = jnp.dot(ckn, wkv_ref[...], preferred_element_type=F32)
    k_rope = (kr * cos + kr_rot * sin).astype(BF16)
    for h in range(MLA_HEADS):
        sl = slice(128 * h, 128 * (h + 1))
        q_ref[h, :, :128] = (qn[:, sl] * scale).astype(BF16)
        q_ref[h, :, 128:] = ((qr[:, sl] * cos + qt[:, sl] * sin) * scale).astype(BF16)
        k_ref[h, :, :128] = kn[:, sl].astype(BF16)
        k_ref[h, :, 128:] = k_rope
        v_ref[h] = vv[:, sl].T.astype(BF16)

    z = jnp.dot(u, wz_ref[...], preferred_element_type=F32)
    z_ref[...] = (z / (1.0 + jnp.exp(-z))).astype(BF16)

    prw = jnp.dot(u, wrw_ref[...], preferred_element_type=F32)

    if row0 == 0:
        @pl.when(i % tiles_per_seq == 0)
        def _():
            shift_ref[0:8, :] = jnp.zeros((8, RW_SHIFT_COLS), F32)

        @pl.when(i % tiles_per_seq != 0)
        def _():
            shift_ref[7:8, :] = shift_ref[tile_rows + 7:tile_rows + 8, :]

    shift_ref[row0 + 8:row0 + tm + 8, :] = prw
    prev = shift_ref[row0 + 7:row0 + tm + 7, :]
    ps = prw + (prev - prw) * mu_ref[...]
    r = ps[:, :RW_WIDTH]
    k = ps[:, RW_WIDTH:2 * RW_WIDTH]
    v = ps[:, 2 * RW_WIDTH:3 * RW_WIDTH]
    xw = ps[:, 3 * RW_WIDTH:3 * RW_WIDTH + W_LORA]
    xa = ps[:, 3 * RW_WIDTH + W_LORA:]

    t = -(w0_ref[...] + jnp.dot(jnp.tanh(xw).astype(BF16), w2_ref[...], preferred_element_type=F32))
    softplus = jnp.maximum(t, 0.0) + jnp.log1p(jnp.exp(-jnp.abs(t)))
    w_log = -softplus - 0.5
    lw_ref[...] = -jnp.exp(w_log)
    a_pre = a0_ref[...] + jnp.dot(xa.astype(BF16), a2_ref[...], preferred_element_type=F32)
    a = 1.0 / (1.0 + jnp.exp(-a_pre))
    kk = k * kkw_ref[...]
    sq = kk * kk
    ss = jnp.dot(sq.astype(BF16), ones_ref[...], preferred_element_type=F32)
    kk = kk / jnp.maximum(jnp.sqrt(ss), 1e-12)
    r_ref[...] = r
    kp_ref[...] = k * (1.0 + (a - 1.0) * ka_ref[...])
    vr_ref[...] = v
    kk_ref[...] = kk
    b_ref[...] = kk * a


def _attn_kernel(q_ref, k_ref, vt_ref, o_ref):
    tq = TQ
    seq = q_ref.shape[1]
    key = lax.broadcasted_iota(jnp.int32, (tq, tq), 0)
    qry = lax.broadcasted_iota(jnp.int32, (tq, tq), 1)
    causal = qry >= key

    def scores(i):
        q = q_ref[0, i * tq:(i + 1) * tq, :]
        blocks = [lax.dot_general(k_ref[0, j * tq:(j + 1) * tq, :], q, (((1,), (1,)), ((), ())),
                                  preferred_element_type=F32) for j in range(i + 1)]
        blocks[i] = jnp.where(causal, blocks[i], -jnp.inf)
        return blocks

    nq = seq // tq
    ahead = [scores(i) for i in range(ATTN_AHEAD)]
    for i in range(nq):
        e = (i + 1) * tq
        blocks = ahead.pop(0)
        if i + ATTN_AHEAD < nq:
            ahead.append(scores(i + ATTN_AHEAD))
        m = blocks[0]
        for sb in blocks[1:]:
            m = jnp.maximum(m, sb)
        m = jnp.max(m, axis=0, keepdims=True)
        p = [jnp.exp2(sb - m) for sb in blocks]
        l = p[0]
        for pb in p[1:]:
            l = l + pb
        l = jnp.sum(l, axis=0, keepdims=True)
        pcat = jnp.concatenate([pb.astype(BF16) for pb in p], axis=0)
        ot = jnp.dot(vt_ref[0, :, :e], pcat, preferred_element_type=F32)
        o_ref[i * tq:e, :] = (ot / l).T.astype(o_ref.dtype)


def _bdot(a, b):
    return jnp.dot(a, b, preferred_element_type=F32)


def _bdot_nt(a, b):
    return lax.dot_general(a, b, (((1,), (1,)), ((), ())), preferred_element_type=F32)


def _wkv_kernel(r_ref, lw_ref, k_ref, v_ref, kk_ref, b_ref, tri_ref, rk_ref, g_ref, bb_ref,
                y_ref, h_ref):
    c = CHUNK
    nseq, tm = r_ref.shape[0], r_ref.shape[1]
    nc = tm // c
    npair = RW_WIDTH // 128
    probs = [(s, ci, q) for s in range(nseq) for ci in range(nc) for q in range(npair)]
    bf = lambda x: x.astype(BF16)

    @pl.when(pl.program_id(1) == 0)
    def _():
        h_ref[...] = jnp.zeros(h_ref.shape, F32)

    tri = tri_ref[...]
    v, r_t, kap_t, k_h, b_h, bonus_w, zbar, p_end = [], [], [], [], [], [], [], []
    for s in range(nseq):
        lw = lw_ref[s]
        lw1 = lw.astype(BF16)
        rem = lw - lw1.astype(F32)
        lw2 = rem.astype(BF16)
        lw3 = (rem - lw2.astype(F32)).astype(BF16)
        cum = _bdot(tri, lw1) + _bdot(tri, lw2) + _bdot(tri, lw3)
        r = r_ref[s]
        k = k_ref[s]
        b = b_ref[s]
        e_neg = jnp.exp(-cum)
        v.append(v_ref[s])
        r_t.append(r * jnp.exp(cum))
        kap_t.append(kk_ref[s] * jnp.exp(cum - lw))
        k_h.append(k * e_neg)
        b_h.append(b * e_neg)
        bonus_w.append(r * k * rk_ref[...])
        zb, pe = [], []
        for ci in range(nc):
            rows = slice(ci * c, (ci + 1) * c)
            cum_end = cum[(ci + 1) * c - 1:(ci + 1) * c, :]
            e_bar = jnp.exp(cum_end - cum[rows])
            zb.append(jnp.concatenate([k[rows] * e_bar, -(b[rows] * e_bar)], axis=0))
            pe.append(jnp.exp(cum_end))
        zbar.append(zb)
        p_end.append(pe)

    i1 = lax.broadcasted_iota(jnp.int32, (c, 128), 0)
    j1 = lax.broadcasted_iota(jnp.int32, (c, 128), 1) % c
    i2 = lax.broadcasted_iota(jnp.int32, (2 * c, 128), 0)
    l2 = lax.broadcasted_iota(jnp.int32, (2 * c, 128), 1)
    j2 = l2 % c
    bd = (i2 // c) == (l2 // c)
    low2 = ((i2 < c) & (i2 > j2)) | (i2 - c >= j2)
    first = j1 == lax.broadcasted_iota(jnp.int32, (c, 128), 1)
    eye = (i1 == j1).astype(F32)
    d8 = (i1 // 8) == (j1 // 8)
    offs = []
    blk = 8
    while blk < c:
        offs.append(((i1 // (2 * blk)) == (j1 // (2 * blk))) & ((i1 // blk) != (j1 // blk)))
        blk *= 2

    def psl(q):
        return slice(128 * q, 128 * (q + 1))

    def rsl(ci):
        return slice(ci * c, (ci + 1) * c)

    def blockdiag(x):
        return bf(jnp.where(bd, jnp.concatenate([x, x], axis=0), 0.0))

    def pick(x):
        return jnp.where(first, x[:c], x[c:])

    sub = lambda a, p: a[p[0]][rsl(p[1]), psl(p[2])]

    x_cat = {p: bf(jnp.concatenate([sub(kap_t, p), sub(r_t, p)], axis=0)) for p in probs}
    a_b = {p: jnp.where(low2, _bdot_nt(x_cat[p], blockdiag(sub(b_h, p))), 0.0) for p in probs}
    a_k = {p: bf(jnp.where(low2, _bdot_nt(x_cat[p], blockdiag(sub(k_h, p))), 0.0)) for p in probs}
    a_kb = {p: a_b[p][:c] for p in probs}
    a_rb = {p: bf(a_b[p][c:]) for p in probs}

    d = {p: jnp.where(d8, a_kb[p], 0.0) for p in probs}
    d2 = {p: _bdot(bf(d[p]), blockdiag(d[p])) for p in probs}
    d4 = {p: _bdot(bf(d2[p]), blockdiag(d2[p])) for p in probs}
    t = {p: _bdot(bf(eye - d[p]), blockdiag(eye + d2[p])) for p in probs}
    t = {p: _bdot(bf(t[p]), blockdiag(eye + d4[p])) for p in probs}
    for off in offs:
        ta = {p: _bdot(bf(t[p]), blockdiag(jnp.where(off, a_kb[p], 0.0))) for p in probs}
        t = {p: t[p] - _bdot(bf(ta[p]), blockdiag(t[p])) for p in probs}
    tb = {p: bf(t[p]) for p in probs}

    vbd = {p: blockdiag(sub(v, p)) for p in probs}
    akv = {p: _bdot(a_k[p], vbd[p]) for p in probs}
    w1 = {p: _bdot(tb[p], blockdiag(sub(kap_t, p))) for p in probs}
    w2 = {p: _bdot(tb[p], blockdiag(akv[p][:c])) for p in probs}
    r2 = {p: sub(r_t, p) - _bdot(a_rb[p], blockdiag(w1[p])) for p in probs}
    y2 = {p: akv[p][c:] - _bdot(a_rb[p], blockdiag(w2[p])) for p in probs}
    zt = {p: bf(zbar[p[0]][p[1]][:, psl(p[2])].T) for p in probs}
    g_mat = {p: pick(_bdot(zt[p], bf(jnp.concatenate([sub(v, p), w2[p]], axis=0)))) for p in probs}
    m_mat = {p: jnp.where(eye > 0, p_end[p[0]][p[1]][:, psl(p[2])], 0.0)
             + pick(_bdot(zt[p], bf(jnp.concatenate([jnp.zeros_like(w1[p]), w1[p]], axis=0)))) for p in probs}
    rm = {p: bf(jnp.concatenate([r2[p], m_mat[p]], axis=0)) for p in probs}

    def seg_mean(x):
        s0 = jnp.sum(jnp.where(first, x, 0.0), axis=-1, keepdims=True)
        s1 = jnp.sum(jnp.where(first, 0.0, x), axis=-1, keepdims=True)
        return jnp.where(first, s0, s1) * (1.0 / c)

    hs = {(s, q): h_ref[s, q] for s in range(nseq) for q in range(npair)}
    for ci in range(nc):
        for s in range(nseq):
            for q in range(npair):
                p = (s, ci, q)
                yh = _bdot(rm[p], blockdiag(hs[(s, q)]))
                hs[(s, q)] = yh[c:] + g_mat[p]
                yy = yh[:c] + y2[p]
                yc = yy - seg_mean(yy)
                var = seg_mean(yc * yc)
                yn = yc * lax.rsqrt(var + RW_GN_EPS) * g_ref[:, psl(q)] + bb_ref[:, psl(q)]
                bonus = seg_mean(sub(bonus_w, p)) * float(c) * sub(v, p)
                y_ref[s, rsl(ci), psl(q)] = (yn + bonus).astype(y_ref.dtype)
    for s in range(nseq):
        for q in range(npair):
            h_ref[s, q] = hs[(s, q)]


def _out_kernel(ym_ref, yr_ref, gate_ref, x_ref, wo_ref, g_ref, o_ref):
    y = jnp.concatenate([ym_ref[...], yr_ref[...]], axis=-1) * gate_ref[...]
    out = jnp.dot(y, wo_ref[...], preferred_element_type=F32)
    o_ref[...] = x_ref[...] + _rms(out, g_ref[...])


def _full(shape):
    nd = len(shape)
    return pl.BlockSpec(shape, lambda *_: (0,) * nd)


def _rot_cols(w):
    half = w.shape[-1] // 2
    return jnp.concatenate([-w[..., half:], w[..., :half]], axis=-1)


def kernel(x, positions, norm_pre_g, w_in, mla_q_norm_g, mla_w_uq, mla_kv_norm_g, mla_w_ukv,
           rw_mu, rw_w0, rw_w2, rw_a0, rw_a2, rw_k_k, rw_k_a, rw_r_k, rw_ln_g, rw_ln_b,
           w_out, norm_post_g):
    bsz, seq, _ = x.shape
    n_tok = bsz * seq
    assert norm_pre_g.shape[0] == 1
    assert seq % TM_PROJ == 0 and seq % TQ == 0 and seq % TM_WKV == 0 and n_tok % TM_OUT == 0
    assert bsz % SEQ_WKV == 0
    row = lambda p: p.reshape(1, -1).astype(F32)

    x2 = x.reshape(n_tok, D_MODEL)
    pos = positions.reshape(n_tok // TM_PROJ, 1, TM_PROJ).astype(F32)
    inv_freq = ROPE_THETA ** (-jnp.arange(0, MLA_ROPE, 2, dtype=F32) / MLA_ROPE)
    invf = inv_freq.reshape(MLA_ROPE // 2, 1)

    w = w_in[0]
    w_kr = w[:, Q_LORA + KV_LORA:MLA_COLS]
    zeros64 = jnp.zeros((D_MODEL, 64), F32)
    wa = jnp.concatenate([w[:, :Q_LORA + KV_LORA], w_kr, zeros64, _rot_cols(w_kr), zeros64], axis=1).astype(BF16)
    wrw = w[:, MLA_COLS:MLA_COLS + RW_SHIFT_COLS].astype(BF16)
    wz = w[:, MLA_COLS + RW_SHIFT_COLS:].astype(BF16)
    wq = mla_w_uq[0].reshape(Q_LORA, MLA_HEADS, MLA_NOPE + MLA_ROPE)
    wqn = wq[:, :, :MLA_NOPE].reshape(Q_LORA, MLA_HEADS * 128).astype(BF16)
    wq_rope = wq[:, :, MLA_NOPE:]
    pad64 = jnp.zeros((Q_LORA, MLA_HEADS, 64), F32)
    wqr = jnp.concatenate([wq_rope, pad64], axis=-1).reshape(Q_LORA, MLA_HEADS * 128).astype(BF16)
    wqt = jnp.concatenate([_rot_cols(wq_rope), pad64], axis=-1).reshape(Q_LORA, MLA_HEADS * 128).astype(BF16)
    wkv = mla_w_ukv[0].reshape(KV_LORA, MLA_HEADS, MLA_NOPE + MLA_V)
    wkn = wkv[:, :, :MLA_NOPE].reshape(KV_LORA, MLA_HEADS * 128).astype(BF16)
    wkvv = wkv[:, :, MLA_NOPE:].reshape(KV_LORA, MLA_HEADS * 128).astype(BF16)
    head_id = np.arange(RW_WIDTH) // RW_HEAD
    ones_bd = jnp.asarray(head_id[:, None] == head_id[None, :], dtype=BF16)

    tm = TM_PROJ
    tok = lambda cols: pl.BlockSpec((tm, cols), lambda i: (i, 0))
    head_major = lambda cols: pl.BlockSpec((MLA_HEADS, tm, cols), lambda i: (0, i, 0))
    proj_in = [x2, pos, row(norm_pre_g), wa, wrw, wz,
               row(mla_q_norm_g), wqn, wqr, wqt, row(mla_kv_norm_g), wkn, wkvv, invf,
               row(rw_mu), row(rw_w0), rw_w2[0].astype(BF16), row(rw_a0), rw_a2[0].astype(BF16),
               row(rw_k_k), row(rw_k_a), ones_bd]
    proj_in_specs = ([tok(D_MODEL), pl.BlockSpec((1, 1, tm), lambda i: (i, 0, 0))]
                     + [_full(a.shape) for a in proj_in[2:]])
    rw_shape = jax.ShapeDtypeStruct((n_tok, RW_WIDTH), F32)
    q, k, v, r, lw, kp, vr, kk, b, z = pl.pallas_call(
        functools.partial(_proj_kernel, seq // tm),
        grid=(n_tok // tm,),
        in_specs=proj_in_specs,
        out_specs=[head_major(QK_PAD), head_major(QK_PAD),
                   pl.BlockSpec((MLA_HEADS, MLA_V, tm), lambda i: (0, 0, i))]
                  + [tok(RW_WIDTH)] * 6 + [tok(D_MODEL)],
        out_shape=[jax.ShapeDtypeStruct((MLA_HEADS, n_tok, QK_PAD), BF16),
                   jax.ShapeDtypeStruct((MLA_HEADS, n_tok, QK_PAD), BF16),
                   jax.ShapeDtypeStruct((MLA_HEADS, MLA_V, n_tok), BF16)]
                  + [rw_shape] * 6 + [jax.ShapeDtypeStruct((n_tok, D_MODEL), BF16)],
        scratch_shapes=[pltpu.VMEM((tm + 8, RW_SHIFT_COLS), F32)],
        compiler_params=pltpu.CompilerParams(dimension_semantics=("arbitrary",),
                                             vmem_limit_bytes=VMEM_LIMIT),
        name="proj",
    )(*proj_in)

    y_mla = pl.pallas_call(
        _attn_kernel,
        grid=(bsz, MLA_HEADS),
        in_specs=[pl.BlockSpec((1, seq, QK_PAD), lambda bi, h: (h, bi, 0)),
                  pl.BlockSpec((1, seq, QK_PAD), lambda bi, h: (h, bi, 0)),
                  pl.BlockSpec((1, MLA_V, seq), lambda bi, h: (h, 0, bi))],
        out_specs=pl.BlockSpec((seq, MLA_V), lambda bi, h: (bi, h)),
        out_shape=jax.ShapeDtypeStruct((n_tok, MLA_WIDTH), BF16),
        compiler_params=pltpu.CompilerParams(dimension_semantics=("arbitrary",) * 2,
                                             vmem_limit_bytes=VMEM_LIMIT),
        name="attn",
    )(q, k, v)

    nt = seq // TM_WKV
    tok_id = np.arange(TM_WKV)
    tri = jnp.asarray((tok_id[:, None] >= tok_id[None, :]) & (tok_id[:, None] // CHUNK == tok_id[None, :] // CHUNK),
                      dtype=BF16)
    tile_spec = pl.BlockSpec((SEQ_WKV, TM_WKV, RW_WIDTH), lambda bi, ti: (bi, ti, 0))
    per_seq = lambda a: a.reshape(bsz, seq, RW_WIDTH)
    wkv_in = [per_seq(a) for a in (r, lw, kp, vr, kk, b)] + [tri, row(rw_r_k), row(rw_ln_g), row(rw_ln_b)]
    y_rw = pl.pallas_call(
        _wkv_kernel,
        grid=(bsz // SEQ_WKV, nt),
        in_specs=[tile_spec] * 6 + [_full(a.shape) for a in wkv_in[6:]],
        out_specs=tile_spec,
        out_shape=jax.ShapeDtypeStruct((bsz, seq, RW_WIDTH), BF16),
        scratch_shapes=[pltpu.VMEM((SEQ_WKV, RW_WIDTH // 128, RW_HEAD, 128), F32)],
        compiler_params=pltpu.CompilerParams(dimension_semantics=("arbitrary", "arbitrary"),
                                             vmem_limit_bytes=VMEM_LIMIT),
        name="wkv",
    )(*wkv_in)

    tmo = TM_OUT
    toko = lambda cols: pl.BlockSpec((tmo, cols), lambda i: (i, 0))
    out = pl.pallas_call(
        _out_kernel,
        grid=(n_tok // tmo,),
        in_specs=[toko(MLA_WIDTH), toko(RW_WIDTH), toko(D_MODEL), toko(D_MODEL),
                  _full((D_MODEL, D_MODEL)), _full((1, D_MODEL))],
        out_specs=toko(D_MODEL),
        out_shape=jax.ShapeDtypeStruct((n_tok, D_MODEL), F32),
        compiler_params=pltpu.CompilerParams(dimension_semantics=("arbitrary",),
                                             vmem_limit_bytes=VMEM_LIMIT),
        name="outproj",
    )(y_mla, y_rw.reshape(n_tok, RW_WIDTH), z, x2, w_out[0].astype(BF16), row(norm_post_g))
    return out.reshape(bsz, seq, D_MODEL)
```

```python
import functools

import jax
import jax.numpy as jnp
import numpy as np
from jax import lax
from jax.experimental import pallas as pl
from jax.experimental.pallas import tpu as pltpu

D_MODEL = 1024
MLA_HEADS = 4
MLA_NOPE = 128
MLA_ROPE = 64
MLA_V = 128
MLA_WIDTH = MLA_HEADS * MLA_V
Q_LORA = 256
KV_LORA = 128
ROPE_THETA = 10000.0
RW_HEAD = 64
RW_WIDTH = 512
RW_HEADS = 8
W_LORA = 64
A_LORA = 64
RW_GN_EPS = 64e-5
NORM_EPS = 1e-6
MLA_COLS = Q_LORA + KV_LORA + MLA_ROPE
RW_SHIFT_COLS = 3 * RW_WIDTH + W_LORA + A_LORA
QK_PAD = 256
WA_COLS = Q_LORA + KV_LORA + 128 + 128

CHUNK = 64
TM_WKV = 256
SEQ_WKV = 2
TM_PROJ = 512
PROJ_SPLIT = 2
TQ = 256
TM_OUT = 1024
VMEM_LIMIT = 48 * 1024 * 1024

F32 = jnp.float32
BF16 = jnp.bfloat16


def _rms(x, g):
    return x * lax.rsqrt(jnp.mean(x * x, axis=-1, keepdims=True) + NORM_EPS) * g


def _proj_kernel(tiles_per_seq, x_ref, pos_ref, *refs):
    params, outs, shift_ref = refs[:20], refs[20:30], refs[30]
    tm = x_ref.shape[0]
    th = tm // PROJ_SPLIT
    for g in range(PROJ_SPLIT):
        rows = pl.ds(g * th, th)
        head_major = [o.at[:, rows] for o in outs[:3]]
        token_major = [o.at[rows] for o in outs[3:]]
        _proj_rows(tiles_per_seq, g * th, tm, x_ref.at[rows], pos_ref.at[:, :, rows], *params,
                   *head_major, *token_major, shift_ref)


def _proj_rows(tiles_per_seq, row0, tile_rows,
               x_ref, pos_ref, gpre_ref, wa_ref, wrw_ref, wz_ref,
               qg_ref, wqn_ref, wqr_ref, wqt_ref, kvg_ref, wkn_ref, wkv_ref, invf_ref,
               mu_ref, w0_ref, w2_ref, a0_ref, a2_ref, kkw_ref, ka_ref, ones_ref,
               q_ref, k_ref, v_ref, r_ref, lw_ref, kp_ref, vr_ref, kk_ref, b_ref, z_ref,
               shift_ref):
    tm = x_ref.shape[0]
    i = pl.program_id(0)
    scale = float((MLA_NOPE + MLA_ROPE) ** -0.5 * np.log2(np.e))

    u = _rms(x_ref[...], gpre_ref[...]).astype(BF16)

    pa = jnp.dot(u, wa_ref[...], preferred_element_type=F32)
    c_q = pa[:, :Q_LORA]
    c_kv = pa[:, Q_LORA:Q_LORA + KV_LORA]
    kr = pa[:, Q_LORA + KV_LORA:Q_LORA + KV_LORA + 128]
    kr_rot = pa[:, Q_LORA + KV_LORA + 128:]

    ang = invf_ref[...] * pos_ref[0]
    cos_t = jnp.cos(ang)
    sin_t = jnp.sin(ang)
    cos = jnp.concatenate([cos_t] * 4, axis=0).T
    sin = jnp.concatenate([sin_t] * 4, axis=0).T

    cqn = _rms(c_q, qg_ref[...]).astype(BF16)
    qn = jnp.dot(cqn, wqn_ref[...], preferred_element_type=F32)
    qr = jnp.dot(cqn, wqr_ref[...], preferred_element_type=F32)
    qt = jnp.dot(cqn, wqt_ref[...], preferred_element_type=F32)
    ckn = _rms(c_kv, kvg_ref[...]).astype(BF16)
    kn = jnp.dot(ckn, wkn_ref[...], preferred_element_type=F32)
    vv = jnp.dot(ckn, wkv_ref[...], preferred_element_type=F32)
    k_rope = (kr * cos + kr_rot * sin).astype(BF16)
    for h in range(MLA_HEADS):
        sl = slice(128 * h, 128 * (h + 1))
        q_ref[h, :, :128] = (qn[:, sl] * scale).astype(BF16)
        q_ref[h, :, 128:] = ((qr[:, sl] * cos + qt[:, sl] * sin) * scale).astype(BF16)
        k_ref[h, :, :128] = kn[:, sl].astype(BF16)
        k_ref[h, :, 128:] = k_rope
        v_ref[h] = vv[:, sl].astype(BF16)

    z = jnp.dot(u, wz_ref[...], preferred_element_type=F32)
    z_ref[...] = (z / (1.0 + jnp.exp(-z))).astype(BF16)

    prw = jnp.dot(u, wrw_ref[...], preferred_element_type=F32)

    if row0 == 0:
        @pl.when(i % tiles_per_seq == 0)
        def _():
            shift_ref[0:8, :] = jnp.zeros((8, RW_SHIFT_COLS), F32)

        @pl.when(i % tiles_per_seq != 0)
        def _():
            shift_ref[7:8, :] = shift_ref[tile_rows + 7:tile_rows + 8, :]

    shift_ref[row0 + 8:row0 + tm + 8, :] = prw
    prev = shift_ref[row0 + 7:row0 + tm + 7, :]
    ps = prw + (prev - prw) * mu_ref[...]
    r = ps[:, :RW_WIDTH]
    k = ps[:, RW_WIDTH:2 * RW_WIDTH]
    v = ps[:, 2 * RW_WIDTH:3 * RW_WIDTH]
    xw = ps[:, 3 * RW_WIDTH:3 * RW_WIDTH + W_LORA]
    xa = ps[:, 3 * RW_WIDTH + W_LORA:]

    t = -(w0_ref[...] + jnp.dot(jnp.tanh(xw).astype(BF16), w2_ref[...], preferred_element_type=F32))
    softplus = jnp.maximum(t, 0.0) + jnp.log1p(jnp.exp(-jnp.abs(t)))
    w_log = -softplus - 0.5
    lw_ref[...] = -jnp.exp(w_log)
    a_pre = a0_ref[...] + jnp.dot(xa.astype(BF16), a2_ref[...], preferred_element_type=F32)
    a = 1.0 / (1.0 + jnp.exp(-a_pre))
    kk = k * kkw_ref[...]
    sq = kk * kk
    ss = jnp.dot(sq.astype(BF16), ones_ref[...], preferred_element_type=F32)
    kk = kk / jnp.maximum(jnp.sqrt(ss), 1e-12)
    r_ref[...] = r
    kp_ref[...] = k * (1.0 + (a - 1.0) * ka_ref[...])
    vr_ref[...] = v
    kk_ref[...] = kk
    b_ref[...] = kk * a


def _attn_kernel(q_ref, k_ref, v_ref, o_ref):
    tq = TQ
    seq = q_ref.shape[1]
    row = lax.broadcasted_iota(jnp.int32, (tq, tq), 0)
    col = lax.broadcasted_iota(jnp.int32, (tq, tq), 1)
    causal = row >= col
    def scores(i):
        q = q_ref[0, i * tq:(i + 1) * tq, :]
        blocks = [lax.dot_general(q, k_ref[0, j * tq:(j + 1) * tq, :], (((1,), (1,)), ((), ())),
                                  preferred_element_type=F32) for j in range(i + 1)]
        blocks[i] = jnp.where(causal, blocks[i], -jnp.inf)
        return blocks

    nq = seq // tq
    ahead = scores(0)
    for i in range(nq):
        e = (i + 1) * tq
        blocks = ahead
        if i + 1 < nq:
            ahead = scores(i + 1)
        m = blocks[0]
        for sb in blocks[1:]:
            m = jnp.maximum(m, sb)
        m = jnp.max(m, axis=-1, keepdims=True)
        p = [jnp.exp2(sb - m) for sb in blocks]
        l = p[0]
        for pb in p[1:]:
            l = l + pb
        l = jnp.sum(l, axis=-1, keepdims=True)
        pcat = jnp.concatenate([pb.astype(BF16) for pb in p], axis=-1)
        o = jnp.dot(pcat, v_ref[0, :e, :], preferred_element_type=F32)
        o_ref[i * tq:e, :] = (o / l).astype(o_ref.dtype)


def _bdot(a, b):
    return jnp.dot(a, b, preferred_element_type=F32)


def _bdot_nt(a, b):
    return lax.dot_general(a, b, (((1,), (1,)), ((), ())), preferred_element_type=F32)


def _wkv_kernel(r_ref, lw_ref, k_ref, v_ref, kk_ref, b_ref, tri_ref, rk_ref, g_ref, bb_ref,
                y_ref, h_ref):
    c = CHUNK
    nseq, tm = r_ref.shape[0], r_ref.shape[1]
    nc = tm // c
    npair = RW_WIDTH // 128
    probs = [(s, ci, q) for s in range(nseq) for ci in range(nc) for q in range(npair)]
    bf = lambda x: x.astype(BF16)

    @pl.when(pl.program_id(1) == 0)
    def _():
        h_ref[...] = jnp.zeros(h_ref.shape, F32)

    tri = tri_ref[...]
    v, r_t, kap_t, k_h, b_h, bonus_w, zbar, p_end = [], [], [], [], [], [], [], []
    for s in range(nseq):
        lw = lw_ref[s]
        lw1 = lw.astype(BF16)
        lw2 = (lw - lw1.astype(F32)).astype(BF16)
        cum = _bdot(tri, lw1) + _bdot(tri, lw2)
        r = r_ref[s]
        k = k_ref[s]
        b = b_ref[s]
        e_neg = jnp.exp(-cum)
        v.append(v_ref[s])
        r_t.append(r * jnp.exp(cum))
        kap_t.append(kk_ref[s] * jnp.exp(cum - lw))
        k_h.append(k * e_neg)
        b_h.append(b * e_neg)
        bonus_w.append(r * k * rk_ref[...])
        zb, pe = [], []
        for ci in range(nc):
            rows = slice(ci * c, (ci + 1) * c)
            cum_end = cum[(ci + 1) * c - 1:(ci + 1) * c, :]
            e_bar = jnp.exp(cum_end - cum[rows])
            zb.append(jnp.concatenate([k[rows] * e_bar, -(b[rows] * e_bar)], axis=0))
            pe.append(jnp.exp(cum_end))
        zbar.append(zb)
        p_end.append(pe)

    i1 = lax.broadcasted_iota(jnp.int32, (c, 128), 0)
    j1 = lax.broadcasted_iota(jnp.int32, (c, 128), 1) % c
    i2 = lax.broadcasted_iota(jnp.int32, (2 * c, 128), 0)
    l2 = lax.broadcasted_iota(jnp.int32, (2 * c, 128), 1)
    j2 = l2 % c
    bd = (i2 // c) == (l2 // c)
    low2 = ((i2 < c) & (i2 > j2)) | (i2 - c >= j2)
    first = j1 == lax.broadcasted_iota(jnp.int32, (c, 128), 1)
    eye = (i1 == j1).astype(F32)
    d8 = (i1 // 8) == (j1 // 8)
    offs = []
    blk = 8
    while blk < c:
        offs.append(((i1 // (2 * blk)) == (j1 // (2 * blk))) & ((i1 // blk) != (j1 // blk)))
        blk *= 2

    def psl(q):
        return slice(128 * q, 128 * (q + 1))

    def rsl(ci):
        return slice(ci * c, (ci + 1) * c)

    def blockdiag(x):
        return bf(jnp.where(bd, jnp.concatenate([x, x], axis=0), 0.0))

    def pick(x):
        return jnp.where(first, x[:c], x[c:])

    sub = lambda a, p: a[p[0]][rsl(p[1]), psl(p[2])]

    x_cat = {p: bf(jnp.concatenate([sub(kap_t, p), sub(r_t, p)], axis=0)) for p in probs}
    a_b = {p: jnp.where(low2, _bdot_nt(x_cat[p], blockdiag(sub(b_h, p))), 0.0) for p in probs}
    a_k = {p: bf(jnp.where(low2, _bdot_nt(x_cat[p], blockdiag(sub(k_h, p))), 0.0)) for p in probs}
    a_kb = {p: a_b[p][:c] for p in probs}
    a_rb = {p: bf(a_b[p][c:]) for p in probs}

    d = {p: jnp.where(d8, a_kb[p], 0.0) for p in probs}
    d2 = {p: _bdot(bf(d[p]), blockdiag(d[p])) for p in probs}
    d4 = {p: _bdot(bf(d2[p]), blockdiag(d2[p])) for p in probs}
    t = {p: _bdot(bf(eye - d[p]), blockdiag(eye + d2[p])) for p in probs}
    t = {p: _bdot(bf(t[p]), blockdiag(eye + d4[p])) for p in probs}
    for off in offs:
        ta = {p: _bdot(bf(t[p]), blockdiag(jnp.where(off, a_kb[p], 0.0))) for p in probs}
        t = {p: t[p] - _bdot(bf(ta[p]), blockdiag(t[p])) for p in probs}
    tb = {p: bf(t[p]) for p in probs}

    vbd = {p: blockdiag(sub(v, p)) for p in probs}
    akv = {p: _bdot(a_k[p], vbd[p]) for p in probs}
    w1 = {p: _bdot(tb[p], blockdiag(sub(kap_t, p))) for p in probs}
    w2 = {p: _bdot(tb[p], blockdiag(akv[p][:c])) for p in probs}
    r2 = {p: sub(r_t, p) - _bdot(a_rb[p], blockdiag(w1[p])) for p in probs}
    y2 = {p: akv[p][c:] - _bdot(a_rb[p], blockdiag(w2[p])) for p in probs}
    zt = {p: bf(zbar[p[0]][p[1]][:, psl(p[2])].T) for p in probs}
    g_mat = {p: pick(_bdot(zt[p], bf(jnp.concatenate([sub(v, p), w2[p]], axis=0)))) for p in probs}
    m_mat = {p: jnp.where(eye > 0, p_end[p[0]][p[1]][:, psl(p[2])], 0.0)
             + pick(_bdot(zt[p], bf(jnp.concatenate([jnp.zeros_like(w1[p]), w1[p]], axis=0)))) for p in probs}
    rm = {p: bf(jnp.concatenate([r2[p], m_mat[p]], axis=0)) for p in probs}

    def seg_mean(x):
        s0 = jnp.sum(jnp.where(first, x, 0.0), axis=-1, keepdims=True)
        s1 = jnp.sum(jnp.where(first, 0.0, x), axis=-1, keepdims=True)
        return jnp.where(first, s0, s1) * (1.0 / c)

    hs = {(s, q): h_ref[s, q] for s in range(nseq) for q in range(npair)}
    for ci in range(nc):
        for s in range(nseq):
            for q in range(npair):
                p = (s, ci, q)
                yh = _bdot(rm[p], blockdiag(hs[(s, q)]))
                hs[(s, q)] = yh[c:] + g_mat[p]
                yy = yh[:c] + y2[p]
                yc = yy - seg_mean(yy)
                var = seg_mean(yc * yc)
                yn = yc * lax.rsqrt(var + RW_GN_EPS) * g_ref[:, psl(q)] + bb_ref[:, psl(q)]
                bonus = seg_mean(sub(bonus_w, p)) * float(c) * sub(v, p)
                y_ref[s, rsl(ci), psl(q)] = (yn + bonus).astype(y_ref.dtype)
    for s in range(nseq):
        for q in range(npair):
            h_ref[s, q] = hs[(s, q)]


def _out_kernel(ym_ref, yr_ref, gate_ref, x_ref, wo_ref, g_ref, o_ref):
    y = jnp.concatenate([ym_ref[...], yr_ref[...]], axis=-1) * gate_ref[...]
    out = jnp.dot(y, wo_ref[...], preferred_element_type=F32)
    o_ref[...] = x_ref[...] + _rms(out, g_ref[...])


def _full(shape):
    nd = len(shape)
    return pl.BlockSpec(shape, lambda *_: (0,) * nd)


def _rot_cols(w):
    half = w.shape[-1] // 2
    return jnp.concatenate([-w[..., half:], w[..., :half]], axis=-1)


def kernel(x, positions, norm_pre_g, w_in, mla_q_norm_g, mla_w_uq, mla_kv_norm_g, mla_w_ukv,
           rw_mu, rw_w0, rw_w2, rw_a0, rw_a2, rw_k_k, rw_k_a, rw_r_k, rw_ln_g, rw_ln_b,
           w_out, norm_post_g):
    bsz, seq, _ = x.shape
    n_tok = bsz * seq
    assert norm_pre_g.shape[0] == 1
    assert seq % TM_PROJ == 0 and seq % TQ == 0 and seq % TM_WKV == 0 and n_tok % TM_OUT == 0
    assert bsz % SEQ_WKV == 0
    row = lambda p: p.reshape(1, -1).astype(F32)

    x2 = x.reshape(n_tok, D_MODEL)
    pos = positions.reshape(n_tok // TM_PROJ, 1, TM_PROJ).astype(F32)
    inv_freq = ROPE_THETA ** (-jnp.arange(0, MLA_ROPE, 2, dtype=F32) / MLA_ROPE)
    invf = inv_freq.reshape(MLA_ROPE // 2, 1)

    w = w_in[0]
    w_kr = w[:, Q_LORA + KV_LORA:MLA_COLS]
    zeros64 = jnp.zeros((D_MODEL, 64), F32)
    wa = jnp.concatenate([w[:, :Q_LORA + KV_LORA], w_kr, zeros64, _rot_cols(w_kr), zeros64], axis=1).astype(BF16)
    wrw = w[:, MLA_COLS:MLA_COLS + RW_SHIFT_COLS].astype(BF16)
    wz = w[:, MLA_COLS + RW_SHIFT_COLS:].astype(BF16)
    wq = mla_w_uq[0].reshape(Q_LORA, MLA_HEADS, MLA_NOPE + MLA_ROPE)
    wqn = wq[:, :, :MLA_NOPE].reshape(Q_LORA, MLA_HEADS * 128).astype(BF16)
    wq_rope = wq[:, :, MLA_NOPE:]
    pad64 = jnp.zeros((Q_LORA, MLA_HEADS, 64), F32)
    wqr = jnp.concatenate([wq_rope, pad64], axis=-1).reshape(Q_LORA, MLA_HEADS * 128).astype(BF16)
    wqt = jnp.concatenate([_rot_cols(wq_rope), pad64], axis=-1).reshape(Q_LORA, MLA_HEADS * 128).astype(BF16)
    wkv = mla_w_ukv[0].reshape(KV_LORA, MLA_HEADS, MLA_NOPE + MLA_V)
    wkn = wkv[:, :, :MLA_NOPE].reshape(KV_LORA, MLA_HEADS * 128).astype(BF16)
    wkvv = wkv[:, :, MLA_NOPE:].reshape(KV_LORA, MLA_HEADS * 128).astype(BF16)
    head_id = np.arange(RW_WIDTH) // RW_HEAD
    ones_bd = jnp.asarray(head_id[:, None] == head_id[None, :], dtype=BF16)

    tm = TM_PROJ
    tok = lambda cols: pl.BlockSpec((tm, cols), lambda i: (i, 0))
    head_major = lambda cols: pl.BlockSpec((MLA_HEADS, tm, cols), lambda i: (0, i, 0))
    proj_in = [x2, pos, row(norm_pre_g), wa, wrw, wz,
               row(mla_q_norm_g), wqn, wqr, wqt, row(mla_kv_norm_g), wkn, wkvv, invf,
               row(rw_mu), row(rw_w0), rw_w2[0].astype(BF16), row(rw_a0), rw_a2[0].astype(BF16),
               row(rw_k_k), row(rw_k_a), ones_bd]
    proj_in_specs = ([tok(D_MODEL), pl.BlockSpec((1, 1, tm), lambda i: (i, 0, 0))]
                     + [_full(a.shape) for a in proj_in[2:]])
    rw_shape = jax.ShapeDtypeStruct((n_tok, RW_WIDTH), F32)
    q, k, v, r, lw, kp, vr, kk, b, z = pl.pallas_call(
        functools.partial(_proj_kernel, seq // tm),
        grid=(n_tok // tm,),
        in_specs=proj_in_specs,
        out_specs=[head_major(QK_PAD), head_major(QK_PAD), head_major(MLA_V)]
                  + [tok(RW_WIDTH)] * 6 + [tok(D_MODEL)],
        out_shape=[jax.ShapeDtypeStruct((MLA_HEADS, n_tok, QK_PAD), BF16),
                   jax.ShapeDtypeStruct((MLA_HEADS, n_tok, QK_PAD), BF16),
                   jax.ShapeDtypeStruct((MLA_HEADS, n_tok, MLA_V), BF16)]
                  + [rw_shape] * 6 + [jax.ShapeDtypeStruct((n_tok, D_MODEL), BF16)],
        scratch_shapes=[pltpu.VMEM((tm + 8, RW_SHIFT_COLS), F32)],
        compiler_params=pltpu.CompilerParams(dimension_semantics=("arbitrary",),
                                             vmem_limit_bytes=VMEM_LIMIT),
        name="proj",
    )(*proj_in)

    y_mla = pl.pallas_call(
        _attn_kernel,
        grid=(bsz, MLA_HEADS),
        in_specs=[pl.BlockSpec((1, seq, QK_PAD), lambda bi, h: (h, bi, 0)),
                  pl.BlockSpec((1, seq, QK_PAD), lambda bi, h: (h, bi, 0)),
                  pl.BlockSpec((1, seq, MLA_V), lambda bi, h: (h, bi, 0))],
        out_specs=pl.BlockSpec((seq, MLA_V), lambda bi, h: (bi, h)),
        out_shape=jax.ShapeDtypeStruct((n_tok, MLA_WIDTH), BF16),
        compiler_params=pltpu.CompilerParams(dimension_semantics=("arbitrary",) * 2,
                                             vmem_limit_bytes=VMEM_LIMIT),
        name="attn",
    )(q, k, v)

    nt = seq // TM_WKV
    tok_id = np.arange(TM_WKV)
    tri = jnp.asarray((tok_id[:, None] >= tok_id[None, :]) & (tok_id[:, None] // CHUNK == tok_id[None, :] // CHUNK),
                      dtype=BF16)
    tile_spec = pl.BlockSpec((SEQ_WKV, TM_WKV, RW_WIDTH), lambda bi, ti: (bi, ti, 0))
    per_seq = lambda a: a.reshape(bsz, seq, RW_WIDTH)
    wkv_in = [per_seq(a) for a in (r, lw, kp, vr, kk, b)] + [tri, row(rw_r_k), row(rw_ln_g), row(rw_ln_b)]
    y_rw = pl.pallas_call(
        _wkv_kernel,
        grid=(bsz // SEQ_WKV, nt),
        in_specs=[tile_spec] * 6 + [_full(a.shape) for a in wkv_in[6:]],
        out_specs=tile_spec,
        out_shape=jax.ShapeDtypeStruct((bsz, seq, RW_WIDTH), BF16),
        scratch_shapes=[pltpu.VMEM((SEQ_WKV, RW_WIDTH // 128, RW_HEAD, 128), F32)],
        compiler_params=pltpu.CompilerParams(dimension_semantics=("arbitrary", "arbitrary"),
                                             vmem_limit_bytes=VMEM_LIMIT),
        name="wkv",
    )(*wkv_in)

    tmo = TM_OUT
    toko = lambda cols: pl.BlockSpec((tmo, cols), lambda i: (i, 0))
    out = pl.pallas_call(
        _out_kernel,
        grid=(n_tok // tmo,),
        in_specs=[toko(MLA_WIDTH), toko(RW_WIDTH), toko(D_MODEL), toko(D_MODEL),
                  _full((D_MODEL, D_MODEL)), _full((1, D_MODEL))],
        out_specs=toko(D_MODEL),
        out_shape=jax.ShapeDtypeStruct((n_tok, D_MODEL), F32),
        compiler_params=pltpu.CompilerParams(dimension_semantics=("arbitrary",),
                                             vmem_limit_bytes=VMEM_LIMIT),
        name="outproj",
    )(y_mla, y_rw.reshape(n_tok, RW_WIDTH), z, x2, w_out[0].astype(BF16), row(norm_post_g))
    return out.reshape(bsz, seq, D_MODEL)
```
